```python
import jax, jax.numpy as jnp
from jax import lax
import numpy as np

D_MODEL = 4096
BATCH = 4
SEQ = 2048
DEPTH = 1

MEM_LEN = 256
M_HEADS = 8
M_QK_DIM = 256
M_V_DIM = 512
M_QK = M_HEADS * M_QK_DIM
M_V = M_HEADS * M_V_DIM
CHUNK = 64
F_BIAS_LO = 3.0
F_BIAS_HI = 6.0
CONV_WIDTH = 2048
CONV_K = 3
X_HEADS = 4
X_HEAD_DIM = 512
X_W = X_HEADS * X_HEAD_DIM
N_BRANCH = 3
SPLIT_SIZES = (M_QK, M_QK, M_V, M_V, M_V, M_HEADS, M_HEADS,
               CONV_WIDTH, CONV_WIDTH, CONV_WIDTH, CONV_WIDTH,
               X_W, X_W, N_BRANCH * D_MODEL)
VALUE_SLOTS = (2, 9)
F_SLOT = 6
D_IN = sum(SPLIT_SIZES)
DEEPNORM_ALPHA = (2 * DEPTH) ** 0.25
DEEPNORM_BETA = (8 * DEPTH) ** -0.25
LN_EPS = 1e-5

kernel_name = "hybrid_mlstm_shortconv_memxattn_deepnorm"


def _layernorm(x, w, b):
    xf = x.astype(jnp.float32)
    mu = xf.mean(-1, keepdims=True)
    var = jnp.mean(jnp.square(xf - mu), -1, keepdims=True)
    return ((xf - mu) * lax.rsqrt(var + LN_EPS)).astype(x.dtype) * w + b


def _mlstm_chunk(carry, inp):
    c_st, n_st, m_st = carry
    q, k, v, ig, lf = inp
    L = q.shape[2]
    b = jnp.cumsum(lf, axis=-1)
    g = b[..., -1]
    causal = jnp.tril(jnp.ones((L, L), dtype=bool))
    log_d = jnp.where(causal, b[..., :, None] - b[..., None, :] + ig[..., None, :], -jnp.inf)
    m_inter = b + m_st[..., None]
    m_t = jnp.maximum(log_d.max(-1), m_inter)
    s = jnp.einsum('bhtk,bhjk->bhtj', q, k) * jnp.exp(log_d - m_t[..., None])
    inter = jnp.exp(m_inter - m_t)
    num = jnp.einsum('bhtj,bhjv->bhtv', s, v) + inter[..., None] * jnp.einsum('bhtk,bhkv->bhtv', q, c_st)
    den = s.sum(-1) + inter * jnp.einsum('bhtk,bhk->bht', q, n_st)
    h = num / jnp.maximum(jnp.abs(den), jnp.exp(-m_t))[..., None]
    w = g[..., None] - b + ig
    m_new = jnp.maximum(g + m_st, w.max(-1))
    decay = jnp.exp(g + m_st - m_new)
    wk = jnp.exp(w - m_new[..., None])[..., None] * k
    c_new = decay[..., None, None] * c_st + jnp.einsum('bhjk,bhjv->bhkv', wk, v)
    n_new = decay[..., None] * n_st + wk.sum(2)
    return (c_new, n_new, m_new), h


def _mlstm(q, k, v, i_pre, f_pre):
    Bsz, S, H, dk = q.shape
    dv = v.shape[-1]
    nc = S // CHUNK
    f32 = jnp.float32

    def to_chunks(t):
        t = t.astype(f32).reshape((Bsz, nc, CHUNK, H) + t.shape[3:])
        return jnp.moveaxis(jnp.moveaxis(t, 1, 0), 3, 2)

    qc = to_chunks(q) * (dk ** -0.5)
    kc, vc = to_chunks(k), to_chunks(v)
    ic = to_chunks(i_pre)
    fc = jax.nn.log_sigmoid(to_chunks(f_pre))
    init = (jnp.zeros((Bsz, H, dk, dv), f32), jnp.zeros((Bsz, H, dk), f32), jnp.zeros((Bsz, H), f32))
    _, hs = lax.scan(_mlstm_chunk, init, (qc, kc, vc, ic, fc))
    hs = jnp.moveaxis(jnp.moveaxis(hs, 0, 1), 3, 2)
    return hs.reshape(Bsz, S, H, dv)


def _layer(x, mem, w_in, b_in, conv_w, mh_norm_w, w_mem_kv, w_proj_m, w_proj_c, w_proj_x, w_out, ln_w, ln_b):
    Bsz, S, _ = x.shape
    u = x @ w_in + b_in
    idx = [int(t) for t in np.cumsum(SPLIT_SIZES)[:-1]]
    mq, mk, mv, mo, mz, mi, mf, cb, cc, cx, cz, xq, xz, gates = jnp.split(u, idx, axis=-1)

    hm = _mlstm(mq.reshape(Bsz, S, M_HEADS, M_QK_DIM), mk.reshape(Bsz, S, M_HEADS, M_QK_DIM),
                mv.reshape(Bsz, S, M_HEADS, M_V_DIM), mi, mf)
    mu = hm.mean(-1, keepdims=True)
    var = jnp.mean(jnp.square(hm - mu), -1, keepdims=True)
    hm = (hm - mu) * lax.rsqrt(var + LN_EPS)
    hm = hm.reshape(Bsz, S, M_V).astype(x.dtype) * mh_norm_w
    y_m = hm * jax.nn.sigmoid(mo) * jax.nn.silu(mz)

    conv_out = lax.conv_general_dilated(cc * cx, conv_w[:, None, :], window_strides=(1,),
                                        padding=[(CONV_K - 1, 0)],
                                        dimension_numbers=('NWC', 'WIO', 'NWC'),
                                        feature_group_count=CONV_WIDTH)
    y_c = cb * conv_out * jax.nn.silu(cz)

    kv = mem @ w_mem_kv
    k_mem, v_mem = jnp.split(kv, 2, axis=-1)
    q = xq.reshape(Bsz, S, X_HEADS, X_HEAD_DIM)
    k_mem = k_mem.reshape(Bsz, -1, X_HEADS, X_HEAD_DIM)
    v_mem = v_mem.reshape(Bsz, -1, X_HEADS, X_HEAD_DIM)
    scores = jnp.einsum('bshd,bmhd->bhsm', q, k_mem).astype(jnp.float32) * (X_HEAD_DIM ** -0.5)
    p = jax.nn.softmax(scores, axis=-1).astype(x.dtype)
    attn = jnp.einsum('bhsm,bmhd->bshd', p, v_mem).reshape(Bsz, S, X_W)
    y_x = attn * jax.nn.silu(xz)

    g_m, g_c, g_x = jnp.split(jax.nn.sigmoid(gates), N_BRANCH, axis=-1)
    merged = g_m * (y_m @ w_proj_m) + g_c * (y_c @ w_proj_c) + g_x * (y_x @ w_proj_x)
    out = merged @ w_out
    return _layernorm(DEEPNORM_ALPHA * x + out, ln_w, ln_b)


def setup_inputs(seed: int = 0) -> dict:
    key = jax.random.key(seed)
    ks = jax.random.split(key, 14)
    nrm = jax.random.normal
    beta = DEEPNORM_BETA
    col_scale = jnp.concatenate([jnp.full((s,), beta if i in VALUE_SLOTS else 1.0, jnp.float32)
                                 for i, s in enumerate(SPLIT_SIZES)])
    w_in = nrm(ks[0], (DEPTH, D_MODEL, D_IN), jnp.float32) * (D_MODEL ** -0.5) * col_scale
    f_off = sum(SPLIT_SIZES[:F_SLOT])
    b_in = 0.01 * nrm(ks[1], (DEPTH, D_IN), jnp.float32)
    b_in = b_in.at[:, f_off:f_off + M_HEADS].add(jnp.linspace(F_BIAS_LO, F_BIAS_HI, M_HEADS))
    conv_w = nrm(ks[2], (DEPTH, CONV_K, CONV_WIDTH), jnp.float32) * (CONV_K ** -0.5)
    mh_norm_w = 1.0 + 0.02 * nrm(ks[3], (DEPTH, M_V), jnp.float32)
    kv_scale = jnp.concatenate([jnp.ones((X_W,), jnp.float32), jnp.full((X_W,), beta, jnp.float32)])
    w_mem_kv = nrm(ks[4], (DEPTH, D_MODEL, 2 * X_W), jnp.float32) * (D_MODEL ** -0.5) * kv_scale
    w_proj_m = nrm(ks[5], (DEPTH, M_V, D_MODEL), jnp.float32) * (M_V ** -0.5) * beta
    w_proj_c = nrm(ks[6], (DEPTH, CONV_WIDTH, D_MODEL), jnp.float32) * (CONV_WIDTH ** -0.5) * beta
    w_proj_x = nrm(ks[7], (DEPTH, X_W, D_MODEL), jnp.float32) * (X_W ** -0.5) * beta
    w_out = nrm(ks[8], (DEPTH, D_MODEL, D_MODEL), jnp.float32) * (D_MODEL ** -0.5) * beta
    ln_w = 1.0 + 0.02 * nrm(ks[9], (DEPTH, D_MODEL), jnp.float32)
    ln_b = 0.02 * nrm(ks[10], (DEPTH, D_MODEL), jnp.float32)
    x = nrm(ks[11], (BATCH, SEQ, D_MODEL), jnp.float32)
    mem = nrm(ks[12], (BATCH, MEM_LEN, D_MODEL), jnp.float32)
    return {"x": x, "mem": mem, "w_in": w_in, "b_in": b_in, "conv_w": conv_w,
            "mh_norm_w": mh_norm_w, "w_mem_kv": w_mem_kv, "w_proj_m": w_proj_m,
            "w_proj_c": w_proj_c, "w_proj_x": w_proj_x, "w_out": w_out,
            "ln_w": ln_w, "ln_b": ln_b}


def reference(x, mem, w_in, b_in, conv_w, mh_norm_w, w_mem_kv, w_proj_m, w_proj_c, w_proj_x, w_out, ln_w, ln_b):
    for l in range(DEPTH):
        x = _layer(x, mem, w_in[l], b_in[l], conv_w[l], mh_norm_w[l], w_mem_kv[l],
                   w_proj_m[l], w_proj_c[l], w_proj_x[l], w_out[l], ln_w[l], ln_b[l])
    return x
```

```python
import functools

import jax
import jax.numpy as jnp
from jax import lax
from jax.experimental import pallas as pl
from jax.experimental.pallas import tpu as pltpu

F32 = jnp.float32
BF16 = jnp.bfloat16

D_MODEL = 4096
M_HEADS = 8
M_QK_DIM = 256
M_V_DIM = 512
M_QK = M_HEADS * M_QK_DIM
M_V = M_HEADS * M_V_DIM
CONV_WIDTH = 2048
CONV_K = 3
X_HEADS = 4
X_HEAD_DIM = 512
X_W = X_HEADS * X_HEAD_DIM
N_BRANCH = 3
LN_EPS = 1e-5

OFF_Q = 0
OFF_K = OFF_Q + M_QK
OFF_V = OFF_K + M_QK
OFF_MO = OFF_V + M_V
OFF_MZ = OFF_MO + M_V
OFF_MI = OFF_MZ + M_V
OFF_MF = OFF_MI + M_HEADS
OFF_CB = OFF_MF + M_HEADS
OFF_CC = OFF_CB + CONV_WIDTH
OFF_CX = OFF_CC + CONV_WIDTH
OFF_CZ = OFF_CX + CONV_WIDTH
OFF_XQ = OFF_CZ + CONV_WIDTH
OFF_XZ = OFF_XQ + X_W
OFF_G = OFF_XZ + X_W
D_IN = OFF_G + N_BRANCH * D_MODEL

V7X_LANES = 128
V7X_VMEM_LIMIT = 56 * 1024 * 1024
MLSTM_CHUNK = 256


def _sigmoid(x):
    return 1.0 / (1.0 + jnp.exp(-x))


def _silu(x):
    return x * _sigmoid(x)


def _log_sigmoid(x):
    return jnp.minimum(x, 0.0) - jnp.log1p(jnp.exp(-jnp.abs(x)))


def _proj_body(*refs, n_w, epilogue):
    x_ref = refs[0]
    w_refs = refs[1:1 + n_w]
    b_refs = refs[1 + n_w:1 + 2 * n_w]
    o_refs = refs[1 + 2 * n_w:]
    x = x_ref[...]
    us = [jnp.dot(x, w[...], preferred_element_type=F32) + b[...] for w, b in zip(w_refs, b_refs)]
    for o, val in zip(o_refs, epilogue(*us)):
        o[...] = val.astype(o.dtype)


def _proj(x, ws, bs, epilogue, out_dtypes, *, bm, bn, name):
    m, k = x.shape
    n = ws[0].shape[1]
    n_w = len(ws)
    grid = (m // bm, n // bn)
    in_specs = ([pl.BlockSpec((bm, k), lambda i, j: (i, 0))]
                + [pl.BlockSpec((k, bn), lambda i, j: (0, j)) for _ in ws]
                + [pl.BlockSpec((1, bn), lambda i, j: (0, j)) for _ in bs])
    out_specs = [pl.BlockSpec((bm, bn), lambda i, j: (i, j)) for _ in out_dtypes]
    out_shape = [jax.ShapeDtypeStruct((m, n), dt) for dt in out_dtypes]
    return pl.pallas_call(
        functools.partial(_proj_body, n_w=n_w, epilogue=epilogue),
        grid=grid, in_specs=in_specs, out_specs=out_specs, out_shape=out_shape,
        compiler_params=pltpu.CompilerParams(
            dimension_semantics=("parallel", "arbitrary"), vmem_limit_bytes=V7X_VMEM_LIMIT),
        name=name,
    )(x, *ws, *bs)


def _xattn_body(q_ref, k_ref, v_ref, z_ref, o_ref):
    scale = X_HEAD_DIM ** -0.5
    for h in range(X_HEADS):
        sl = slice(h * X_HEAD_DIM, (h + 1) * X_HEAD_DIM)
        s = lax.dot_general(q_ref[:, sl], k_ref[:, sl], (((1,), (1,)), ((), ())),
                            preferred_element_type=F32) * scale
        e = jnp.exp(s - jnp.max(s, axis=-1, keepdims=True))
        p = e / jnp.sum(e, axis=-1, keepdims=True)
        a = jnp.dot(p.astype(BF16), v_ref[:, sl], preferred_element_type=F32)
        o_ref[:, sl] = (a * z_ref[:, sl].astype(F32)).astype(o_ref.dtype)


def _xattn(xq, sxz, kv, *, batch, seq, mem_len, bs):
    nb = seq // bs
    return pl.pallas_call(
        _xattn_body,
        grid=(batch, nb),
        in_specs=[pl.BlockSpec((bs, X_W), lambda b, i: (b * nb + i, 0)),
                  pl.BlockSpec((mem_len, X_W), lambda b, i: (b, 0)),
                  pl.BlockSpec((mem_len, X_W), lambda b, i: (b, 1)),
                  pl.BlockSpec((bs, X_W), lambda b, i: (b * nb + i, 0))],
        out_specs=pl.BlockSpec((bs, X_W), lambda b, i: (b * nb + i, 0)),
        out_shape=jax.ShapeDtypeStruct((batch * seq, X_W), BF16),
        compiler_params=pltpu.CompilerParams(
            dimension_semantics=("parallel", "arbitrary"), vmem_limit_bytes=V7X_VMEM_LIMIT),
        name="xattn",
    )(xq, kv, kv, sxz)


def _mlstm_body(q_ref, k_ref, v_ref, g_ref, gate_ref, nw_ref, o_ref, c_ref, n_ref, m_ref):
    L = MLSTM_CHUNK
    h = pl.program_id(1)

    @pl.when(pl.program_id(2) == 0)
    def _():
        c_ref[...] = jnp.zeros_like(c_ref)
        n_ref[...] = jnp.zeros_like(n_ref)
        m_ref[...] = jnp.zeros_like(m_ref)

    scale = M_QK_DIM ** -0.5
    q = q_ref[...]
    k = k_ref[...]
    v = v_ref[...]
    ig = g_ref[pl.ds(h, 1), :]
    lf = _log_sigmoid(g_ref[pl.ds(M_HEADS + h, 1), :])

    t_idx = lax.broadcasted_iota(jnp.int32, (L, L), 0)
    j_idx = lax.broadcasted_iota(jnp.int32, (L, L), 1)
    causal = j_idx <= t_idx
    eye = j_idx == t_idx
    b_col = jnp.sum(jnp.where(causal, lf, 0.0), axis=1, keepdims=True)
    b_row = jnp.sum(jnp.where(eye, b_col, 0.0), axis=0, keepdims=True)
    i_col = jnp.sum(jnp.where(eye, ig, 0.0), axis=1, keepdims=True)
    g = jnp.sum(lf, axis=1, keepdims=True)
    m_prev = m_ref[:, 0:1]

    log_d = jnp.where(causal, b_col - b_row + ig, -jnp.inf)
    m_inter = b_col + m_prev
    m_t = jnp.maximum(jnp.max(log_d, axis=1, keepdims=True), m_inter)
    s = lax.dot_general(q, k, (((1,), (1,)), ((), ())), preferred_element_type=F32)
    s = s * scale * jnp.exp(log_d - m_t)
    inter = jnp.exp(m_inter - m_t) * scale
    c_st = c_ref[...]
    num = (jnp.dot(s.astype(BF16), v, preferred_element_type=F32)
           + inter * jnp.dot(q, c_st.astype(BF16), preferred_element_type=F32))
    qn = jnp.sum(q.astype(F32) * n_ref[...], axis=1, keepdims=True)
    den = jnp.sum(s, axis=1, keepdims=True) + inter * qn
    hh = num / jnp.maximum(jnp.abs(den), jnp.exp(-m_t))

    w_col = g - b_col + i_col
    m_new = jnp.maximum(g + m_prev, jnp.max(w_col, axis=0, keepdims=True))
    decay = jnp.exp(g + m_prev - m_new)
    wk = jnp.exp(w_col - m_new) * k.astype(F32)
    c_ref[...] = decay * c_st + lax.dot_general(
        wk.astype(BF16), v, (((0,), (0,)), ((), ())), preferred_element_type=F32)
    n_ref[...] = decay * n_ref[...] + jnp.sum(wk, axis=0, keepdims=True)
    m_ref[...] = jnp.broadcast_to(m_new, m_ref.shape)

    mu = jnp.mean(hh, axis=1, keepdims=True)
    hc = hh - mu
    var = jnp.mean(hc * hc, axis=1, keepdims=True)
    y = hc * lax.rsqrt(var + LN_EPS) * nw_ref[...] * gate_ref[...].astype(F32)
    o_ref[...] = y.astype(o_ref.dtype)


def _mlstm(qkv, gates_t, gate_m, norm_w, *, batch, seq):
    L = MLSTM_CHUNK
    nc = seq // L
    kq = M_QK // M_QK_DIM
    kv = (2 * M_QK) // M_V_DIM
    return pl.pallas_call(
        _mlstm_body,
        grid=(batch, M_HEADS, nc),
        in_specs=[pl.BlockSpec((L, M_QK_DIM), lambda b, h, c: (b * nc + c, h)),
                  pl.BlockSpec((L, M_QK_DIM), lambda b, h, c: (b * nc + c, kq + h)),
                  pl.BlockSpec((L, M_V_DIM), lambda b, h, c: (b * nc + c, kv + h)),
                  pl.BlockSpec((None, 2 * M_HEADS, L), lambda b, h, c: (b, 0, c)),
                  pl.BlockSpec((L, M_V_DIM), lambda b, h, c: (b * nc + c, h)),
                  pl.BlockSpec((1, M_V_DIM), lambda b, h, c: (0, h))],
        out_specs=pl.BlockSpec((L, M_V_DIM), lambda b, h, c: (b * nc + c, h)),
        out_shape=jax.ShapeDtypeStruct((batch * seq, M_V), BF16),
        scratch_shapes=[pltpu.VMEM((M_QK_DIM, M_V_DIM), F32),
                        pltpu.VMEM((1, M_QK_DIM), F32),
                        pltpu.VMEM((1, V7X_LANES), F32)],
        compiler_params=pltpu.CompilerParams(
            dimension_semantics=("parallel", "parallel", "arbitrary"),
            vmem_limit_bytes=V7X_VMEM_LIMIT),
        name="mlstm",
    )(qkv, qkv, qkv, gates_t, gate_m, norm_w)


def _conv_body(p_ref, gc_ref, w_ref, o_ref):
    p = p_ref[...].astype(F32)
    row = lax.broadcasted_iota(jnp.int32, p.shape, 0)
    acc = p * w_ref[CONV_K - 1:CONV_K, :]
    for d in range(1, CONV_K):
        shifted = jnp.where(row >= d, pltpu.roll(p, d, 0), 0.0)
        acc = acc + shifted * w_ref[CONV_K - 1 - d:CONV_K - d, :]
    o_ref[...] = (acc * gc_ref[...].astype(F32)).astype(o_ref.dtype)


def _conv(p, gc, conv_w, *, batch, seq, bn):
    return pl.pallas_call(
        _conv_body,
        grid=(batch, CONV_WIDTH // bn),
        in_specs=[pl.BlockSpec((seq, bn), lambda b, j: (b, j)),
                  pl.BlockSpec((seq, bn), lambda b, j: (b, j)),
                  pl.BlockSpec((CONV_K, bn), lambda b, j: (0, j))],
        out_specs=pl.BlockSpec((seq, bn), lambda b, j: (b, j)),
        out_shape=jax.ShapeDtypeStruct((batch * seq, CONV_WIDTH), BF16),
        compiler_params=pltpu.CompilerParams(
            dimension_semantics=("parallel", "arbitrary"), vmem_limit_bytes=V7X_VMEM_LIMIT),
        name="conv",
    )(p, gc, conv_w)


def _merge_body(ym_ref, yc_ref, yx_ref, wm_ref, wc_ref, wx_ref, gm_ref, gc_ref, gx_ref, o_ref):
    acc = gm_ref[...].astype(F32) * jnp.dot(ym_ref[...], wm_ref[...], preferred_element_type=F32)
    acc = acc + gc_ref[...].astype(F32) * jnp.dot(yc_ref[...], wc_ref[...], preferred_element_type=F32)
    acc = acc + gx_ref[...].astype(F32) * jnp.dot(yx_ref[...], wx_ref[...], preferred_element_type=F32)
    o_ref[...] = acc.astype(o_ref.dtype)


def _merge(y_m, y_c, y_x, wm, wc, wx, gates, *, bm, bn):
    m = y_m.shape[0]
    nb = D_MODEL // bn
    return pl.pallas_call(
        _merge_body,
        grid=(m // bm, nb),
        in_specs=[pl.BlockSpec((bm, M_V), lambda i, j: (i, 0)),
                  pl.BlockSpec((bm, CONV_WIDTH), lambda i, j: (i, 0)),
                  pl.BlockSpec((bm, X_W), lambda i, j: (i, 0)),
                  pl.BlockSpec((M_V, bn), lambda i, j: (0, j)),
                  pl.BlockSpec((CONV_WIDTH, bn), lambda i, j: (0, j)),
                  pl.BlockSpec((X_W, bn), lambda i, j: (0, j)),
                  pl.BlockSpec((bm, bn), lambda i, j: (i, j)),
                  pl.BlockSpec((bm, bn), lambda i, j: (i, nb + j)),
                  pl.BlockSpec((bm, bn), lambda i, j: (i, 2 * nb + j))],
        out_specs=pl.BlockSpec((bm, bn), lambda i, j: (i, j)),
        out_shape=jax.ShapeDtypeStruct((m, D_MODEL), BF16),
        compiler_params=pltpu.CompilerParams(
            dimension_semantics=("parallel", "arbitrary"), vmem_limit_bytes=V7X_VMEM_LIMIT),
        name="merge",
    )(y_m, y_c, y_x, wm, wc, wx, gates, gates, gates)


def _out_body(mg_ref, w_ref, x_ref, lw_ref, lb_ref, o_ref, *, bn, alpha):
    j = pl.program_id(1)
    z = alpha * x_ref[...] + jnp.dot(mg_ref[...], w_ref[...], preferred_element_type=F32)
    o_ref[:, pl.ds(pl.multiple_of(j * bn, bn), bn)] = z

    @pl.when(j == pl.num_programs(1) - 1)
    def _():
        zz = o_ref[...]
        mu = jnp.mean(zz, axis=1, keepdims=True)
        zc = zz - mu
        var = jnp.mean(zc * zc, axis=1, keepdims=True)
        o_ref[...] = zc * lax.rsqrt(var + LN_EPS) * lw_ref[...] + lb_ref[...]


def _out(merged, w_out, x2, ln_w, ln_b, *, bm, bn, alpha):
    m = merged.shape[0]
    return pl.pallas_call(
        functools.partial(_out_body, bn=bn, alpha=alpha),
        grid=(m // bm, D_MODEL // bn),
        in_specs=[pl.BlockSpec((bm, D_MODEL), lambda i, j: (i, 0)),
                  pl.BlockSpec((D_MODEL, bn), lambda i, j: (0, j)),
                  pl.BlockSpec((bm, bn), lambda i, j: (i, j)),
                  pl.BlockSpec((1, D_MODEL), lambda i, j: (0, 0)),
                  pl.BlockSpec((1, D_MODEL), lambda i, j: (0, 0))],
        out_specs=pl.BlockSpec((bm, D_MODEL), lambda i, j: (i, 0)),
        out_shape=jax.ShapeDtypeStruct((m, D_MODEL), F32),
        compiler_params=pltpu.CompilerParams(
            dimension_semantics=("parallel", "arbitrary"), vmem_limit_bytes=V7X_VMEM_LIMIT),
        name="outproj_ln",
    )(merged, w_out, x2, ln_w, ln_b)


def _layer(x, mem, w_in, b_in, conv_w, mh_norm_w, w_mem_kv, w_proj_m, w_proj_c, w_proj_x, w_out,
           ln_w, ln_b, *, alpha):
    batch, seq, d = x.shape
    mem_len = mem.shape[1]
    tokens = batch * seq
    x2 = x.reshape(tokens, d)
    xb = x2.astype(BF16)

    def wseg(off, width):
        return w_in[:, off:off + width].astype(BF16)

    def bseg(off, width):
        return b_in[off:off + width].reshape(1, width)

    ident = lambda u: (u,)
    bm = 1024

    (qkv,) = _proj(xb, [wseg(OFF_Q, 2 * M_QK + M_V)], [bseg(OFF_Q, 2 * M_QK + M_V)], ident, [BF16],
                   bm=bm, bn=512, name="proj_qkv")
    (gate_m,) = _proj(xb, [wseg(OFF_MO, M_V), wseg(OFF_MZ, M_V)], [bseg(OFF_MO, M_V), bseg(OFF_MZ, M_V)],
                      lambda mo, mz: (_sigmoid(mo) * _silu(mz),), [BF16], bm=bm, bn=512, name="proj_mgate")
    n_if = 2 * M_HEADS
    w_if = jnp.pad(wseg(OFF_MI, n_if), ((0, 0), (0, V7X_LANES - n_if)))
    b_if = jnp.pad(bseg(OFF_MI, n_if), ((0, 0), (0, V7X_LANES - n_if)))
    (if_pre,) = _proj(xb, [w_if], [b_if], ident, [F32], bm=bm, bn=V7X_LANES, name="proj_if")
    p_conv, g_conv = _proj(
        xb, [wseg(OFF_CB, CONV_WIDTH), wseg(OFF_CC, CONV_WIDTH), wseg(OFF_CX, CONV_WIDTH), wseg(OFF_CZ, CONV_WIDTH)],
        [bseg(OFF_CB, CONV_WIDTH), bseg(OFF_CC, CONV_WIDTH), bseg(OFF_CX, CONV_WIDTH), bseg(OFF_CZ, CONV_WIDTH)],
        lambda cb, cc, cx, cz: (cc * cx, cb * _silu(cz)), [BF16, BF16], bm=bm, bn=256, name="proj_conv")
    xq, sxz = _proj(xb, [wseg(OFF_XQ, X_W), wseg(OFF_XZ, X_W)], [bseg(OFF_XQ, X_W), bseg(OFF_XZ, X_W)],
                    lambda q, z: (q, _silu(z)), [BF16, BF16], bm=bm, bn=512, name="proj_xattn")
    (gates,) = _proj(xb, [wseg(OFF_G, N_BRANCH * D_MODEL)], [bseg(OFF_G, N_BRANCH * D_MODEL)],
                     lambda u: (_sigmoid(u),), [BF16], bm=bm, bn=512, name="proj_gates")

    mem2 = mem.reshape(batch * mem_len, d).astype(BF16)
    (kv,) = _proj(mem2, [w_mem_kv.astype(BF16)], [jnp.zeros((1, 2 * X_W), F32)], ident, [BF16],
                  bm=batch * mem_len, bn=512, name="proj_memkv")
    y_x = _xattn(xq, sxz, kv, batch=batch, seq=seq, mem_len=mem_len, bs=512)

    gates_t = if_pre[:, :n_if].reshape(batch, seq, n_if).transpose(0, 2, 1)
    y_m = _mlstm(qkv, gates_t, gate_m, mh_norm_w.reshape(1, M_V), batch=batch, seq=seq)

    y_c = _conv(p_conv, g_conv, conv_w, batch=batch, seq=seq, bn=512)

    merged = _merge(y_m, y_c, y_x, w_proj_m.astype(BF16), w_proj_c.astype(BF16), w_proj_x.astype(BF16),
                    gates, bm=512, bn=512)
    out = _out(merged, w_out.astype(BF16), x2, ln_w.reshape(1, d), ln_b.reshape(1, d),
               bm=512, bn=512, alpha=alpha)
    return out.reshape(batch, seq, d)


def kernel(x, mem, w_in, b_in, conv_w, mh_norm_w, w_mem_kv, w_proj_m, w_proj_c, w_proj_x, w_out, ln_w, ln_b):
    depth = w_in.shape[0]
    alpha = (2 * depth) ** 0.25
    for l in range(depth):
        x = _layer(x, mem, w_in[l], b_in[l], conv_w[l], mh_norm_w[l], w_mem_kv[l], w_proj_m[l],
                   w_proj_c[l], w_proj_x[l], w_out[l], ln_w[l], ln_b[l], alpha=alpha)
    return x
```

```python
import functools

import jax
import jax.numpy as jnp
from jax import lax
from jax.experimental import pallas as pl
from jax.experimental.pallas import tpu as pltpu

F32 = jnp.float32
BF16 = jnp.bfloat16

D_MODEL = 4096
M_HEADS = 8
M_QK_DIM = 256
M_V_DIM = 512
M_QK = M_HEADS * M_QK_DIM
M_V = M_HEADS * M_V_DIM
CONV_WIDTH = 2048
CONV_K = 3
X_HEADS = 4
X_HEAD_DIM = 512
X_W = X_HEADS * X_HEAD_DIM
N_BRANCH = 3
LN_EPS = 1e-5

OFF_Q = 0
OFF_K = OFF_Q + M_QK
OFF_V = OFF_K + M_QK
OFF_MO = OFF_V + M_V
OFF_MZ = OFF_MO + M_V
OFF_MI = OFF_MZ + M_V
OFF_MF = OFF_MI + M_HEADS
OFF_CB = OFF_MF + M_HEADS
OFF_CC = OFF_CB + CONV_WIDTH
OFF_CX = OFF_CC + CONV_WIDTH
OFF_CZ = OFF_CX + CONV_WIDTH
OFF_XQ = OFF_CZ + CONV_WIDTH
OFF_XZ = OFF_XQ + X_W
OFF_G = OFF_XZ + X_W
D_IN = OFF_G + N_BRANCH * D_MODEL

V7X_LANES = 128
V7X_VMEM_LIMIT = 56 * 1024 * 1024
MLSTM_CHUNK = 256

SHIFT = OFF_CB - OFF_MI
PAD_W = 256
SEG_W = CONV_WIDTH + PAD_W
GATES_W = N_BRANCH * D_MODEL + 512


def _sigmoid(x):
    return 1.0 / (1.0 + jnp.exp(-x))


def _silu(x):
    return x * _sigmoid(x)


def _log_sigmoid(x):
    return jnp.minimum(x, 0.0) - jnp.log1p(jnp.exp(-jnp.abs(x)))


def _valid_cols(shape, bn, width):
    col = pl.program_id(1) * bn + lax.broadcasted_iota(jnp.int32, shape, 1)
    return (col >= SHIFT) & (col < SHIFT + width)


def _proj_body(*refs, n_w, has_bias, epilogue):
    x_ref = refs[0]
    w_refs = refs[1:1 + n_w]
    n_b = n_w if has_bias else 0
    b_refs = refs[1 + n_w:1 + n_w + n_b]
    o_refs = refs[1 + n_w + n_b:]
    x = x_ref[...]
    us = []
    for idx, w in enumerate(w_refs):
        u = jnp.dot(x, w[...].astype(BF16), preferred_element_type=F32)
        if has_bias:
            u = u + b_refs[idx][...]
        us.append(u)
    epilogue(us, o_refs)


def _proj(x, w, b, layer, col_offs, n_blocks, epilogue, out_dtypes, *, bm, bn, name, row_out=None):
    m, k = x.shape
    n_w = len(col_offs)
    in_specs = [pl.BlockSpec((bm, k), lambda i, j: (i, 0))]
    in_specs += [pl.BlockSpec((None, k, bn), lambda i, j, o=off // bn: (layer, 0, o + j)) for off in col_offs]
    operands = [x] + [w] * n_w
    if b is not None:
        in_specs += [pl.BlockSpec((None, 1, bn), lambda i, j, o=off // bn: (layer, 0, o + j)) for off in col_offs]
        operands += [b] * n_w
    out_specs = [pl.BlockSpec((bm, bn), lambda i, j: (i, j)) for _ in out_dtypes]
    out_shape = [jax.ShapeDtypeStruct((m, n_blocks * bn), dt) for dt in out_dtypes]
    if row_out is not None:
        out_specs.append(pl.BlockSpec((bm, row_out[0]), lambda i, j: (i, 0)))
        out_shape.append(jax.ShapeDtypeStruct((m, row_out[0]), row_out[1]))
    return pl.pallas_call(
        functools.partial(_proj_body, n_w=n_w, has_bias=b is not None, epilogue=epilogue),
        grid=(m // bm, n_blocks), in_specs=in_specs, out_specs=out_specs, out_shape=out_shape,
        compiler_params=pltpu.CompilerParams(
            dimension_semantics=("parallel", "arbitrary"), vmem_limit_bytes=V7X_VMEM_LIMIT),
        name=name,
    )(*operands)


def _ep_plain(us, o_refs):
    o_refs[0][...] = us[0].astype(o_refs[0].dtype)


def _ep_mgate(us, o_refs):
    mo, mz = us
    o_refs[0][...] = (_sigmoid(mo) * _silu(mz)).astype(o_refs[0].dtype)


def _ep_conv_p(us, o_refs, *, bn):
    cc, cx = us
    valid = _valid_cols(cc.shape, bn, CONV_WIDTH)
    o_refs[0][...] = jnp.where(valid, cc * cx, 0.0).astype(o_refs[0].dtype)


def _ep_conv_g(us, o_refs, *, bn):
    cb, cz = us
    valid = _valid_cols(cb.shape, bn, CONV_WIDTH)
    o_refs[0][...] = jnp.where(valid, cb * _silu(cz), 0.0).astype(o_refs[0].dtype)

    @pl.when(pl.program_id(1) == 0)
    def _():
        o_refs[1][...] = cb[:, :V7X_LANES]


def _ep_xattn(us, o_refs, *, bn):
    q, z = us
    valid = _valid_cols(q.shape, bn, X_W)
    o_refs[0][...] = jnp.where(valid, q, 0.0).astype(o_refs[0].dtype)
    o_refs[1][...] = jnp.where(valid, _silu(z), 0.0).astype(o_refs[1].dtype)


def _ep_gates(us, o_refs, *, bn):
    valid = _valid_cols(us[0].shape, bn, N_BRANCH * D_MODEL)
    o_refs[0][...] = jnp.where(valid, _sigmoid(us[0]), 0.0).astype(o_refs[0].dtype)


def _xattn_body(q_ref, k_ref, v_ref, z_ref, o_ref):
    scale = X_HEAD_DIM ** -0.5
    win = X_HEAD_DIM + V7X_LANES
    mem_len = k_ref.shape[0]

    def shifted(a):
        a32 = jnp.concatenate([a.astype(F32), jnp.zeros((mem_len, PAD_W), F32)], axis=1)
        return pltpu.roll(a32, SHIFT, 1)

    ks = shifted(k_ref[...])
    vs = shifted(v_ref[...])
    col = lax.broadcasted_iota(jnp.int32, (mem_len, win), 1)
    head_cols = (col >= SHIFT) & (col < SHIFT + X_HEAD_DIM)

    def gated(a, lo, width):
        return (a * z_ref[:, lo:lo + width].astype(F32)).astype(o_ref.dtype)

    tail = None
    for h in range(X_HEADS):
        lo = h * X_HEAD_DIM
        kh = jnp.where(head_cols, ks[:, lo:lo + win], 0.0).astype(BF16)
        vh = jnp.where(head_cols, vs[:, lo:lo + win], 0.0).astype(BF16)
        s = lax.dot_general(q_ref[:, lo:lo + win], kh, (((1,), (1,)), ((), ())),
                            preferred_element_type=F32) * scale
        e = jnp.exp(s - jnp.max(s, axis=-1, keepdims=True))
        p = e / jnp.sum(e, axis=-1, keepdims=True)
        a = jnp.dot(p.astype(BF16), vh, preferred_element_type=F32)
        first = a[:, :V7X_LANES] if tail is None else a[:, :V7X_LANES] + tail
        o_ref[:, lo:lo + V7X_LANES] = gated(first, lo, V7X_LANES)
        o_ref[:, lo + V7X_LANES:lo + X_HEAD_DIM] = gated(
            a[:, V7X_LANES:X_HEAD_DIM], lo + V7X_LANES, X_HEAD_DIM - V7X_LANES)
        tail = a[:, X_HEAD_DIM:]
    o_ref[:, X_W:X_W + V7X_LANES] = gated(tail, X_W, V7X_LANES)
    o_ref[:, X_W + V7X_LANES:] = jnp.zeros((o_ref.shape[0], PAD_W - V7X_LANES), o_ref.dtype)


def _xattn(xq, sxz, kv, *, batch, seq, mem_len, bs):
    nb = seq // bs
    return pl.pallas_call(
        _xattn_body,
        grid=(batch, nb),
        in_specs=[pl.BlockSpec((bs, SEG_W), lambda b, i: (b * nb + i, 0)),
                  pl.BlockSpec((mem_len, X_W), lambda b, i: (b, 0)),
                  pl.BlockSpec((mem_len, X_W), lambda b, i: (b, 1)),
                  pl.BlockSpec((bs, SEG_W), lambda b, i: (b * nb + i, 0))],
        out_specs=pl.BlockSpec((bs, SEG_W), lambda b, i: (b * nb + i, 0)),
        out_shape=jax.ShapeDtypeStruct((batch * seq, SEG_W), BF16),
        compiler_params=pltpu.CompilerParams(
            dimension_semantics=("parallel", "arbitrary"), vmem_limit_bytes=V7X_VMEM_LIMIT),
        name="xattn",
    )(xq, kv, kv, sxz)


def _mlstm_body(q_ref, k_ref, v_ref, g_ref, gate_ref, nw_ref, o_ref, c_ref, n_ref, m_ref):
    L = MLSTM_CHUNK
    h = pl.program_id(1)

    @pl.when(pl.program_id(2) == 0)
    def _():
        c_ref[...] = jnp.zeros_like(c_ref)
        n_ref[...] = jnp.zeros_like(n_ref)
        m_ref[...] = jnp.zeros_like(m_ref)

    scale = M_QK_DIM ** -0.5
    q = q_ref[...]
    k = k_ref[...]
    v = v_ref[...]
    ig = g_ref[pl.ds(h, 1), :]
    lf = _log_sigmoid(g_ref[pl.ds(M_HEADS + h, 1), :])

    t_idx = lax.broadcasted_iota(jnp.int32, (L, L), 0)
    j_idx = lax.broadcasted_iota(jnp.int32, (L, L), 1)
    causal = j_idx <= t_idx
    eye = j_idx == t_idx
    b_col = jnp.sum(jnp.where(causal, lf, 0.0), axis=1, keepdims=True)
    b_row = jnp.sum(jnp.where(eye, b_col, 0.0), axis=0, keepdims=True)
    i_col = jnp.sum(jnp.where(eye, ig, 0.0), axis=1, keepdims=True)
    g = jnp.sum(lf, axis=1, keepdims=True)
    m_prev = m_ref[:, 0:1]

    log_d = jnp.where(causal, b_col - b_row + ig, -jnp.inf)
    m_inter = b_col + m_prev
    m_t = jnp.maximum(jnp.max(log_d, axis=1, keepdims=True), m_inter)
    s = lax.dot_general(q, k, (((1,), (1,)), ((), ())), preferred_element_type=F32)
    s = s * scale * jnp.exp(log_d - m_t)
    inter = jnp.exp(m_inter - m_t) * scale
    c_st = c_ref[...]
    num = (jnp.dot(s.astype(BF16), v, preferred_element_type=F32)
           + inter * jnp.dot(q, c_st.astype(BF16), preferred_element_type=F32))
    qn = jnp.sum(q.astype(F32) * n_ref[...], axis=1, keepdims=True)
    den = jnp.sum(s, axis=1, keepdims=True) + inter * qn
    hh = num / jnp.maximum(jnp.abs(den), jnp.exp(-m_t))

    w_col = g - b_col + i_col
    m_new = jnp.maximum(g + m_prev, jnp.max(w_col, axis=0, keepdims=True))
    decay = jnp.exp(g + m_prev - m_new)
    wk = jnp.exp(w_col - m_new) * k.astype(F32)
    c_ref[...] = decay * c_st + lax.dot_general(
        wk.astype(BF16), v, (((0,), (0,)), ((), ())), preferred_element_type=F32)
    n_ref[...] = decay * n_ref[...] + jnp.sum(wk, axis=0, keepdims=True)
    m_ref[...] = jnp.broadcast_to(m_new, m_ref.shape)

    mu = jnp.mean(hh, axis=1, keepdims=True)
    hc = hh - mu
    var = jnp.mean(hc * hc, axis=1, keepdims=True)
    y = hc * lax.rsqrt(var + LN_EPS) * nw_ref[...] * gate_ref[...].astype(F32)
    o_ref[...] = y.astype(o_ref.dtype)


def _mlstm(qkv, gates_t, gate_m, norm_w, *, batch, seq):
    L = MLSTM_CHUNK
    nc = seq // L
    kq = M_QK // M_QK_DIM
    kv = (2 * M_QK) // M_V_DIM
    return pl.pallas_call(
        _mlstm_body,
        grid=(batch, M_HEADS, nc),
        in_specs=[pl.BlockSpec((L, M_QK_DIM), lambda b, h, c: (b * nc + c, h)),
                  pl.BlockSpec((L, M_QK_DIM), lambda b, h, c: (b * nc + c, kq + h)),
                  pl.BlockSpec((L, M_V_DIM), lambda b, h, c: (b * nc + c, kv + h)),
                  pl.BlockSpec((None, 2 * M_HEADS, L), lambda b, h, c: (b, 0, c)),
                  pl.BlockSpec((L, M_V_DIM), lambda b, h, c: (b * nc + c, h)),
                  pl.BlockSpec((1, M_V_DIM), lambda b, h, c: (0, h))],
        out_specs=pl.BlockSpec((L, M_V_DIM), lambda b, h, c: (b * nc + c, h)),
        out_shape=jax.ShapeDtypeStruct((batch * seq, M_V), BF16),
        scratch_shapes=[pltpu.VMEM((M_QK_DIM, M_V_DIM), F32),
                        pltpu.VMEM((1, M_QK_DIM), F32),
                        pltpu.VMEM((1, V7X_LANES), F32)],
        compiler_params=pltpu.CompilerParams(
            dimension_semantics=("parallel", "parallel", "arbitrary"),
            vmem_limit_bytes=V7X_VMEM_LIMIT),
        name="mlstm",
    )(qkv, qkv, qkv, gates_t, gate_m, norm_w)


def _conv_body(p_ref, gc_ref, w_ref, o_ref):
    p = p_ref[...].astype(F32)
    row = lax.broadcasted_iota(jnp.int32, p.shape, 0)
    acc = p * w_ref[CONV_K - 1:CONV_K, :]
    for d in range(1, CONV_K):
        shifted = jnp.where(row >= d, pltpu.roll(p, d, 0), 0.0)
        acc = acc + shifted * w_ref[CONV_K - 1 - d:CONV_K - d, :]
    o_ref[...] = (acc * gc_ref[...].astype(F32)).astype(o_ref.dtype)


def _conv(p, gc, conv_w, *, batch, seq, bn):
    width = p.shape[1]
    return pl.pallas_call(
        _conv_body,
        grid=(batch, width // bn),
        in_specs=[pl.BlockSpec((seq, bn), lambda b, j: (b, j)),
                  pl.BlockSpec((seq, bn), lambda b, j: (b, j)),
                  pl.BlockSpec((CONV_K, bn), lambda b, j: (0, j))],
        out_specs=pl.BlockSpec((seq, bn), lambda b, j: (b, j)),
        out_shape=jax.ShapeDtypeStruct((batch * seq, width), BF16),
        compiler_params=pltpu.CompilerParams(
            dimension_semantics=("parallel", "arbitrary"), vmem_limit_bytes=V7X_VMEM_LIMIT),
        name="conv",
    )(p, gc, conv_w)


def _merge_body(ym_ref, yc_ref, yx_ref, wm_ref, wc_ref, wx_ref,
                gm_ref, gm_next, gc_ref, gc_next, gx_ref, gx_next, o_ref):
    bn = o_ref.shape[1]

    def gate(main, nxt):
        g = jnp.concatenate([main[...], nxt[...]], axis=1).astype(F32)
        return g[:, SHIFT:SHIFT + bn]

    acc = gate(gm_ref, gm_next) * jnp.dot(ym_ref[...], wm_ref[...], preferred_element_type=F32)
    acc = acc + gate(gc_ref, gc_next) * jnp.dot(yc_ref[...], wc_ref[...], preferred_element_type=F32)
    acc = acc + gate(gx_ref, gx_next) * jnp.dot(yx_ref[...], wx_ref[...], preferred_element_type=F32)
    o_ref[...] = acc.astype(o_ref.dtype)


def _merge(y_m, y_c, y_x, wm, wc, wx, gates, *, bm, bn):
    m = y_m.shape[0]
    nb = D_MODEL // bn
    sub = bn // V7X_LANES
    gate_specs = []
    for gi in range(N_BRANCH):
        gate_specs.append(pl.BlockSpec((bm, bn), lambda i, j, o=gi * nb: (i, o + j)))
        gate_specs.append(pl.BlockSpec((bm, V7X_LANES), lambda i, j, o=gi * nb: (i, (o + j + 1) * sub)))
    return pl.pallas_call(
        _merge_body,
        grid=(m // bm, nb),
        in_specs=[pl.BlockSpec((bm, M_V), lambda i, j: (i, 0)),
                  pl.BlockSpec((bm, SEG_W), lambda i, j: (i, 0)),
                  pl.BlockSpec((bm, SEG_W), lambda i, j: (i, 0)),
                  pl.BlockSpec((M_V, bn), lambda i, j: (0, j)),
                  pl.BlockSpec((SEG_W, bn), lambda i, j: (0, j)),
                  pl.BlockSpec((SEG_W, bn), lambda i, j: (0, j))] + gate_specs,
        out_specs=pl.BlockSpec((bm, bn), lambda i, j: (i, j)),
        out_shape=jax.ShapeDtypeStruct((m, D_MODEL), BF16),
        compiler_params=pltpu.CompilerParams(
            dimension_semantics=("parallel", "arbitrary"), vmem_limit_bytes=V7X_VMEM_LIMIT),
        name="merge",
    )(y_m, y_c, y_x, wm, wc, wx, *([gates] * (2 * N_BRANCH)))


def _out_body(mg_ref, w_ref, x_ref, lw_ref, lb_ref, o_ref, *, bn, alpha):
    j = pl.program_id(1)
    z = alpha * x_ref[...] + jnp.dot(mg_ref[...], w_ref[...], preferred_element_type=F32)
    o_ref[:, pl.ds(pl.multiple_of(j * bn, bn), bn)] = z

    @pl.when(j == pl.num_programs(1) - 1)
    def _():
        zz = o_ref[...]
        mu = jnp.mean(zz, axis=1, keepdims=True)
        zc = zz - mu
        var = jnp.mean(zc * zc, axis=1, keepdims=True)
        o_ref[...] = zc * lax.rsqrt(var + LN_EPS) * lw_ref[...] + lb_ref[...]


def _out(merged, w_out, x2, ln_w, ln_b, *, bm, bn, alpha):
    m = merged.shape[0]
    return pl.pallas_call(
        functools.partial(_out_body, bn=bn, alpha=alpha),
        grid=(m // bm, D_MODEL // bn),
        in_specs=[pl.BlockSpec((bm, D_MODEL), lambda i, j: (i, 0)),
                  pl.BlockSpec((D_MODEL, bn), lambda i, j: (0, j)),
                  pl.BlockSpec((bm, bn), lambda i, j: (i, j)),
                  pl.BlockSpec((1, D_MODEL), lambda i, j: (0, 0)),
                  pl.BlockSpec((1, D_MODEL), lambda i, j: (0, 0))],
        out_specs=pl.BlockSpec((bm, D_MODEL), lambda i, j: (i, 0)),
        out_shape=jax.ShapeDtypeStruct((m, D_MODEL), F32),
        compiler_params=pltpu.CompilerParams(
            dimension_semantics=("parallel", "arbitrary"), vmem_limit_bytes=V7X_VMEM_LIMIT),
        name="outproj_ln",
    )(merged, w_out, x2, ln_w, ln_b)


def _shift_rows(w):
    return jnp.pad(w, ((SHIFT, PAD_W - SHIFT), (0, 0))).astype(BF16)


def _layer(x, mem, w_in, b_in, layer, conv_w, mh_norm_w, w_mem_kv, w_proj_m, w_proj_c, w_proj_x, w_out,
           ln_w, ln_b, *, alpha):
    batch, seq, d = x.shape
    mem_len = mem.shape[1]
    tokens = batch * seq
    x2 = x.reshape(tokens, d)
    xb = x2.astype(BF16)
    b3 = b_in.reshape(b_in.shape[0], 1, D_IN)
    bm = 1024

    (qkv,) = _proj(xb, w_in, b3, layer, [OFF_Q], (2 * M_QK + M_V) // 512, _ep_plain, [BF16],
                   bm=bm, bn=512, name="proj_qkv")
    (gate_m,) = _proj(xb, w_in, b3, layer, [OFF_MO, OFF_MZ], M_V // 256, _ep_mgate, [BF16],
                      bm=bm, bn=256, name="proj_mgate")
    (p_conv,) = _proj(xb, w_in, b3, layer, [OFF_CC - SHIFT, OFF_CX - SHIFT], SEG_W // 256,
                      functools.partial(_ep_conv_p, bn=256), [BF16], bm=bm, bn=256, name="proj_conv_p")
    g_conv, if_pre = _proj(xb, w_in, b3, layer, [OFF_CB - SHIFT, OFF_CZ - SHIFT], SEG_W // 256,
                           functools.partial(_ep_conv_g, bn=256), [BF16], bm=bm, bn=256,
                           name="proj_conv_g", row_out=(V7X_LANES, F32))
    xq, sxz = _proj(xb, w_in, b3, layer, [OFF_XQ - SHIFT, OFF_XZ - SHIFT], SEG_W // 256,
                    functools.partial(_ep_xattn, bn=256), [BF16, BF16], bm=bm, bn=256, name="proj_xattn")
    (gates,) = _proj(xb, w_in, b3, layer, [OFF_G - SHIFT], GATES_W // 512,
                     functools.partial(_ep_gates, bn=512), [BF16], bm=bm, bn=512, name="proj_gates")

    mem2 = mem.reshape(batch * mem_len, d).astype(BF16)
    (kv,) = _proj(mem2, w_mem_kv, None, layer, [0], (2 * X_W) // 512, _ep_plain, [BF16],
                  bm=batch * mem_len, bn=512, name="proj_memkv")
    y_x = _xattn(xq, sxz, kv, batch=batch, seq=seq, mem_len=mem_len, bs=512)

    n_if = 2 * M_HEADS
    gates_t = if_pre[:, :n_if].reshape(batch, seq, n_if).transpose(0, 2, 1)
    y_m = _mlstm(qkv, gates_t, gate_m, mh_norm_w.reshape(1, M_V), batch=batch, seq=seq)

    conv_ws = jnp.pad(conv_w, ((0, 0), (SHIFT, PAD_W - SHIFT)))
    y_c = _conv(p_conv, g_conv, conv_ws, batch=batch, seq=seq, bn=SEG_W // 3)

    merged = _merge(y_m, y_c, y_x, w_proj_m.astype(BF16), _shift_rows(w_proj_c), _shift_rows(w_proj_x),
                    gates, bm=512, bn=512)
    out = _out(merged, w_out.astype(BF16), x2, ln_w.reshape(1, d), ln_b.reshape(1, d),
               bm=512, bn=512, alpha=alpha)
    return out.reshape(batch, seq, d)


def kernel(x, mem, w_in, b_in, conv_w, mh_norm_w, w_mem_kv, w_proj_m, w_proj_c, w_proj_x, w_out, ln_w, ln_b):
    depth = w_in.shape[0]
    alpha = (2 * depth) ** 0.25
    for l in range(depth):
        x = _layer(x, mem, w_in, b_in, l, conv_w[l], mh_norm_w[l], w_mem_kv, w_proj_m[l],
                   w_proj_c[l], w_proj_x[l], w_out[l], ln_w[l], ln_b[l], alpha=alpha)
    return x
```

```python
import functools

import jax
import jax.numpy as jnp
from jax import lax
from jax.experimental import pallas as pl
from jax.experimental.pallas import tpu as pltpu

F32 = jnp.float32
BF16 = jnp.bfloat16

D_MODEL = 4096
M_HEADS = 8
M_QK_DIM = 256
M_V_DIM = 512
M_QK = M_HEADS * M_QK_DIM
M_V = M_HEADS * M_V_DIM
CONV_WIDTH = 2048
CONV_K = 3
X_HEADS = 4
X_HEAD_DIM = 512
X_W = X_HEADS * X_HEAD_DIM
N_BRANCH = 3
LN_EPS = 1e-5

OFF_Q = 0
OFF_K = OFF_Q + M_QK
OFF_V = OFF_K + M_QK
OFF_MO = OFF_V + M_V
OFF_MZ = OFF_MO + M_V
OFF_MI = OFF_MZ + M_V
OFF_MF = OFF_MI + M_HEADS
OFF_CB = OFF_MF + M_HEADS
OFF_CC = OFF_CB + CONV_WIDTH
OFF_CX = OFF_CC + CONV_WIDTH
OFF_CZ = OFF_CX + CONV_WIDTH
OFF_XQ = OFF_CZ + CONV_WIDTH
OFF_XZ = OFF_XQ + X_W
OFF_G = OFF_XZ + X_W
D_IN = OFF_G + N_BRANCH * D_MODEL

V7X_LANES = 128
V7X_VMEM_LIMIT = 56 * 1024 * 1024
MLSTM_CHUNK = 256
MLSTM_HEAD_GROUP = 2


def _sigmoid(x):
    return 0.5 * jnp.tanh(0.5 * x) + 0.5


def _silu(x):
    return x * _sigmoid(x)


def _log_sigmoid(x):
    return jnp.minimum(x, 0.0) - jnp.log1p(jnp.exp(-jnp.abs(x)))


def _proj_body(*refs, n_w, has_bias, transposed, epilogue):
    x_ref = refs[0]
    w_refs = refs[1:1 + n_w]
    n_b = n_w if has_bias else 0
    b_refs = refs[1 + n_w:1 + n_w + n_b]
    o_refs = refs[1 + n_w + n_b:]
    x = x_ref[...]
    contract = (((1,), (1,)), ((), ())) if transposed else (((1,), (0,)), ((), ()))
    us = []
    for idx, w in enumerate(w_refs):
        u = lax.dot_general(x, w[...].astype(BF16), contract, preferred_element_type=F32)
        if has_bias:
            u = u + b_refs[idx][...]
        us.append(u)
    for o, val in zip(o_refs, epilogue(*us)):
        o[...] = val.astype(o.dtype)


def _proj(x, w, biases, layer, col_offs, width, epilogue, out_dtypes, *, transposed, bm, bn, name):
    m, k = x.shape
    n_w = len(col_offs)
    in_specs = [pl.BlockSpec((bm, k), lambda i, j: (i, 0))]
    if transposed:
        base = layer * w.shape[1]
        w = w.reshape(-1, k)
        in_specs += [pl.BlockSpec((pl.Element(bn), pl.Element(k)), lambda i, j, o=base + off: (pl.multiple_of(o + j * bn, 8), 0))
                     for off in col_offs]
    else:
        in_specs += [pl.BlockSpec((None, k, bn), lambda i, j, o=off // bn: (layer, 0, o + j))
                     for off in col_offs]
    operands = [x] + [w] * n_w
    if biases is not None:
        in_specs += [pl.BlockSpec((1, bn), lambda i, j: (0, j)) for _ in biases]
        operands += list(biases)
    out_specs = [pl.BlockSpec((bm, bn), lambda i, j: (i, j)) for _ in out_dtypes]
    out_shape = [jax.ShapeDtypeStruct((m, width), dt) for dt in out_dtypes]
    return pl.pallas_call(
        functools.partial(_proj_body, n_w=n_w, has_bias=biases is not None, transposed=transposed,
                          epilogue=epilogue),
        grid=(m // bm, width // bn), in_specs=in_specs, out_specs=out_specs, out_shape=out_shape,
        compiler_params=pltpu.CompilerParams(
            dimension_semantics=("parallel", "arbitrary"), vmem_limit_bytes=V7X_VMEM_LIMIT),
        name=name,
    )(*operands)


def _xattn_body(q_ref, k_ref, v_ref, z_ref, o_ref):
    scale = X_HEAD_DIM ** -0.5
    for h in range(X_HEADS):
        sl = slice(h * X_HEAD_DIM, (h + 1) * X_HEAD_DIM)
        s = lax.dot_general(q_ref[:, sl], k_ref[:, sl], (((1,), (1,)), ((), ())),
                            preferred_element_type=F32) * scale
        e = jnp.exp(s - jnp.max(s, axis=-1, keepdims=True))
        p = e / jnp.sum(e, axis=-1, keepdims=True)
        a = jnp.dot(p.astype(BF16), v_ref[:, sl], preferred_element_type=F32)
        o_ref[:, sl] = (a * z_ref[:, sl].astype(F32)).astype(o_ref.dtype)


def _xattn(xq, sxz, kv, *, batch, seq, mem_len, bs):
    nb = seq // bs
    return pl.pallas_call(
        _xattn_body,
        grid=(batch, nb),
        in_specs=[pl.BlockSpec((bs, X_W), lambda b, i: (b * nb + i, 0)),
                  pl.BlockSpec((mem_len, X_W), lambda b, i: (b, 0)),
                  pl.BlockSpec((mem_len, X_W), lambda b, i: (b, 1)),
                  pl.BlockSpec((bs, X_W), lambda b, i: (b * nb + i, 0))],
        out_specs=pl.BlockSpec((bs, X_W), lambda b, i: (b * nb + i, 0)),
        out_shape=jax.ShapeDtypeStruct((batch * seq, X_W), BF16),
        compiler_params=pltpu.CompilerParams(
            dimension_semantics=("parallel", "arbitrary"), vmem_limit_bytes=V7X_VMEM_LIMIT),
        name="xattn",
    )(xq, kv, kv, sxz)


def _mlstm_body(q_ref, k_ref, v_ref, g_ref, gate_ref, nw_ref, o_ref, c_ref, n_ref, m_ref):
    @pl.when(pl.program_id(2) == 0)
    def _():
        c_ref[...] = jnp.zeros_like(c_ref)
        n_ref[...] = jnp.zeros_like(n_ref)
        m_ref[...] = jnp.zeros_like(m_ref)

    for hg in range(MLSTM_HEAD_GROUP):
        qk_cols = slice(hg * M_QK_DIM, (hg + 1) * M_QK_DIM)
        v_cols = slice(hg * M_V_DIM, (hg + 1) * M_V_DIM)
        _mlstm_head(pl.program_id(1) * MLSTM_HEAD_GROUP + hg,
                    q_ref.at[:, qk_cols], k_ref.at[:, qk_cols], v_ref.at[:, v_cols], g_ref,
                    gate_ref.at[:, v_cols], nw_ref.at[:, v_cols], o_ref.at[:, v_cols],
                    c_ref.at[hg], n_ref.at[hg], m_ref.at[hg])


def _mlstm_head(h, q_ref, k_ref, v_ref, g_ref, gate_ref, nw_ref, o_ref, c_ref, n_ref, m_ref):
    L = MLSTM_CHUNK
    scale = M_QK_DIM ** -0.5
    q = q_ref[...]
    k = k_ref[...]
    v = v_ref[...]
    ig = g_ref[pl.ds(h, 1), :]
    lf = _log_sigmoid(g_ref[pl.ds(M_HEADS + h, 1), :])

    t_idx = lax.broadcasted_iota(jnp.int32, (L, L), 0)
    j_idx = lax.broadcasted_iota(jnp.int32, (L, L), 1)
    causal = j_idx <= t_idx
    eye = j_idx == t_idx
    b_col = jnp.sum(jnp.where(causal, lf, 0.0), axis=1, keepdims=True)
    b_row = jnp.sum(jnp.where(eye, b_col, 0.0), axis=0, keepdims=True)
    i_col = jnp.sum(jnp.where(eye, ig, 0.0), axis=1, keepdims=True)
    g = jnp.sum(lf, axis=1, keepdims=True)
    m_prev = m_ref[:, 0:1]

    log_d = jnp.where(causal, b_col - b_row + ig, -jnp.inf)
    m_inter = b_col + m_prev
    m_t = jnp.maximum(jnp.max(log_d, axis=1, keepdims=True), m_inter)
    s = lax.dot_general(q, k, (((1,), (1,)), ((), ())), preferred_element_type=F32)
    s = s * scale * jnp.exp(log_d - m_t)
    inter = jnp.exp(m_inter - m_t) * scale
    c_st = c_ref[...]
    num = (jnp.dot(s.astype(BF16), v, preferred_element_type=F32)
           + inter * jnp.dot(q, c_st.astype(BF16), preferred_element_type=F32))
    qn = jnp.sum(q.astype(F32) * n_ref[...], axis=1, keepdims=True)
    den = jnp.sum(s, axis=1, keepdims=True) + inter * qn
    hh = num / jnp.maximum(jnp.abs(den), jnp.exp(-m_t))

    w_col = g - b_col + i_col
    m_new = jnp.maximum(g + m_prev, jnp.max(w_col, axis=0, keepdims=True))
    decay = jnp.exp(g + m_prev - m_new)
    wk = jnp.exp(w_col - m_new) * k.astype(F32)
    c_ref[...] = decay * c_st + lax.dot_general(
        wk.astype(BF16), v, (((0,), (0,)), ((), ())), preferred_element_type=F32)
    n_ref[...] = decay * n_ref[...] + jnp.sum(wk, axis=0, keepdims=True)
    m_ref[...] = jnp.broadcast_to(m_new, m_ref.shape)

    mu = jnp.mean(hh, axis=1, keepdims=True)
    hc = hh - mu
    var = jnp.mean(hc * hc, axis=1, keepdims=True)
    y = hc * lax.rsqrt(var + LN_EPS) * nw_ref[...] * gate_ref[...].astype(F32)
    o_ref[...] = y.astype(o_ref.dtype)


def _mlstm(qkv, gates_t, gate_m, norm_w, *, batch, seq):
    L = MLSTM_CHUNK
    nc = seq // L
    G = MLSTM_HEAD_GROUP
    qk_w = G * M_QK_DIM
    v_w = G * M_V_DIM
    kq = M_QK // qk_w
    kv = (2 * M_QK) // v_w
    return pl.pallas_call(
        _mlstm_body,
        grid=(batch, M_HEADS // G, nc),
        in_specs=[pl.BlockSpec((L, qk_w), lambda b, h, c: (b * nc + c, h)),
                  pl.BlockSpec((L, qk_w), lambda b, h, c: (b * nc + c, kq + h)),
                  pl.BlockSpec((L, v_w), lambda b, h, c: (b * nc + c, kv + h)),
                  pl.BlockSpec((None, 2 * M_HEADS, L), lambda b, h, c: (b, 0, c)),
                  pl.BlockSpec((L, v_w), lambda b, h, c: (b * nc + c, h)),
                  pl.BlockSpec((1, v_w), lambda b, h, c: (0, h))],
        out_specs=pl.BlockSpec((L, v_w), lambda b, h, c: (b * nc + c, h)),
        out_shape=jax.ShapeDtypeStruct((batch * seq, M_V), BF16),
        scratch_shapes=[pltpu.VMEM((G, M_QK_DIM, M_V_DIM), F32),
                        pltpu.VMEM((G, 1, M_QK_DIM), F32),
                        pltpu.VMEM((G, 1, V7X_LANES), F32)],
        compiler_params=pltpu.CompilerParams(
            dimension_semantics=("parallel", "parallel", "arbitrary"),
            vmem_limit_bytes=V7X_VMEM_LIMIT),
        name="mlstm",
    )(qkv, qkv, qkv, gates_t, gate_m, norm_w)


def _conv_body(p_ref, gc_ref, w_ref, o_ref):
    p = p_ref[...].astype(F32)
    row = lax.broadcasted_iota(jnp.int32, p.shape, 0)
    acc = p * w_ref[CONV_K - 1:CONV_K, :]
    for d in range(1, CONV_K):
        shifted = jnp.where(row >= d, pltpu.roll(p, d, 0), 0.0)
        acc = acc + shifted * w_ref[CONV_K - 1 - d:CONV_K - d, :]
    o_ref[...] = (acc * gc_ref[...].astype(F32)).astype(o_ref.dtype)


def _conv(p, gc, conv_w, *, batch, seq, bn):
    return pl.pallas_call(
        _conv_body,
        grid=(batch, CONV_WIDTH // bn),
        in_specs=[pl.BlockSpec((seq, bn), lambda b, j: (b, j)),
                  pl.BlockSpec((seq, bn), lambda b, j: (b, j)),
                  pl.BlockSpec((CONV_K, bn), lambda b, j: (0, j))],
        out_specs=pl.BlockSpec((seq, bn), lambda b, j: (b, j)),
        out_shape=jax.ShapeDtypeStruct((batch * seq, CONV_WIDTH), BF16),
        compiler_params=pltpu.CompilerParams(
            dimension_semantics=("parallel", "arbitrary"), vmem_limit_bytes=V7X_VMEM_LIMIT),
        name="conv",
    )(p, gc, conv_w)


def _merge_body(ym_ref, yc_ref, yx_ref, wm_ref, wc_ref, wx_ref, gm_ref, gc_ref, gx_ref, o_ref):
    acc = gm_ref[...].astype(F32) * jnp.dot(ym_ref[...], wm_ref[...], preferred_element_type=F32)
    acc = acc + gc_ref[...].astype(F32) * jnp.dot(yc_ref[...], wc_ref[...], preferred_element_type=F32)
    acc = acc + gx_ref[...].astype(F32) * jnp.dot(yx_ref[...], wx_ref[...], preferred_element_type=F32)
    o_ref[...] = acc.astype(o_ref.dtype)


def _merge(y_m, y_c, y_x, wm, wc, wx, gates, *, bm, bn):
    m = y_m.shape[0]
    nb = D_MODEL // bn
    return pl.pallas_call(
        _merge_body,
        grid=(m // bm, nb),
        in_specs=[pl.BlockSpec((bm, M_V), lambda i, j: (i, 0)),
                  pl.BlockSpec((bm, CONV_WIDTH), lambda i, j: (i, 0)),
                  pl.BlockSpec((bm, X_W), lambda i, j: (i, 0)),
                  pl.BlockSpec((M_V, bn), lambda i, j: (0, j)),
                  pl.BlockSpec((CONV_WIDTH, bn), lambda i, j: (0, j)),
                  pl.BlockSpec((X_W, bn), lambda i, j: (0, j)),
                  pl.BlockSpec((bm, bn), lambda i, j: (i, j)),
                  pl.BlockSpec((bm, bn), lambda i, j: (i, nb + j)),
                  pl.BlockSpec((bm, bn), lambda i, j: (i, 2 * nb + j))],
        out_specs=pl.BlockSpec((bm, bn), lambda i, j: (i, j)),
        out_shape=jax.ShapeDtypeStruct((m, D_MODEL), BF16),
        compiler_params=pltpu.CompilerParams(
            dimension_semantics=("parallel", "arbitrary"), vmem_limit_bytes=V7X_VMEM_LIMIT),
        name="merge",
    )(y_m, y_c, y_x, wm, wc, wx, gates, gates, gates)


def _out_body(mg_ref, w_ref, x_ref, lw_ref, lb_ref, o_ref, *, bn, alpha):
    j = pl.program_id(1)
    z = alpha * x_ref[...] + jnp.dot(mg_ref[...], w_ref[...], preferred_element_type=F32)
    o_ref[:, pl.ds(pl.multiple_of(j * bn, bn), bn)] = z

    @pl.when(j == pl.num_programs(1) - 1)
    def _():
        zz = o_ref[...]
        mu = jnp.mean(zz, axis=1, keepdims=True)
        zc = zz - mu
        var = jnp.mean(zc * zc, axis=1, keepdims=True)
        o_ref[...] = zc * lax.rsqrt(var + LN_EPS) * lw_ref[...] + lb_ref[...]


def _out(merged, w_out, x2, ln_w, ln_b, *, bm, bn, alpha):
    m = merged.shape[0]
    return pl.pallas_call(
        functools.partial(_out_body, bn=bn, alpha=alpha),
        grid=(m // bm, D_MODEL // bn),
        in_specs=[pl.BlockSpec((bm, D_MODEL), lambda i, j: (i, 0)),
                  pl.BlockSpec((D_MODEL, bn), lambda i, j: (0, j)),
                  pl.BlockSpec((bm, bn), lambda i, j: (i, j)),
                  pl.BlockSpec((1, D_MODEL), lambda i, j: (0, 0)),
                  pl.BlockSpec((1, D_MODEL), lambda i, j: (0, 0))],
        out_specs=pl.BlockSpec((bm, D_MODEL), lambda i, j: (i, 0)),
        out_shape=jax.ShapeDtypeStruct((m, D_MODEL), F32),
        compiler_params=pltpu.CompilerParams(
            dimension_semantics=("parallel", "arbitrary"), vmem_limit_bytes=V7X_VMEM_LIMIT),
        name="outproj_ln",
    )(merged, w_out, x2, ln_w, ln_b)


def _layer(x, mem, w_in_t, b_in, layer, conv_w, mh_norm_w, w_mem_kv, w_proj_m, w_proj_c, w_proj_x, w_out,
           ln_w, ln_b, *, alpha):
    batch, seq, d = x.shape
    mem_len = mem.shape[1]
    tokens = batch * seq
    x2 = x.reshape(tokens, d)
    xb = x2.astype(BF16)

    def bseg(off, width):
        return b_in[off:off + width].reshape(1, width)

    def proj(offs, width, epilogue, out_dtypes, bn, name):
        return _proj(xb, w_in_t, [bseg(o, width) for o in offs], layer, offs, width, epilogue, out_dtypes,
                     transposed=True, bm=1024, bn=bn, name=name)

    ident = lambda u: (u,)
    (qkv,) = proj([OFF_Q], 2 * M_QK + M_V, ident, [BF16], 512, "proj_qkv")
    (gate_m,) = proj([OFF_MO, OFF_MZ], M_V, lambda mo, mz: (_sigmoid(mo) * _silu(mz),), [BF16], 256,
                     "proj_mgate")
    (if_pre,) = proj([OFF_MI], V7X_LANES, ident, [F32], V7X_LANES, "proj_if")
    (p_conv,) = proj([OFF_CC, OFF_CX], CONV_WIDTH, lambda cc, cx: (cc * cx,), [BF16], 256, "proj_conv_p")
    (g_conv,) = proj([OFF_CB, OFF_CZ], CONV_WIDTH, lambda cb, cz: (cb * _silu(cz),), [BF16], 256,
                     "proj_conv_g")
    xq, sxz = proj([OFF_XQ, OFF_XZ], X_W, lambda q, z: (q, _silu(z)), [BF16, BF16], 256, "proj_xattn")
    (gates,) = proj([OFF_G], N_BRANCH * D_MODEL, lambda u: (_sigmoid(u),), [BF16], 512, "proj_gates")

    mem2 = mem.reshape(batch * mem_len, d).astype(BF16)
    (kv,) = _proj(mem2, w_mem_kv, None, layer, [0], 2 * X_W, ident, [BF16], transposed=False,
                  bm=batch * mem_len, bn=512, name="proj_memkv")
    y_x = _xattn(xq, sxz, kv, batch=batch, seq=seq, mem_len=mem_len, bs=512)

    n_if = 2 * M_HEADS
    gates_t = if_pre[:, :n_if].reshape(batch, seq, n_if).transpose(0, 2, 1)
    y_m = _mlstm(qkv, gates_t, gate_m, mh_norm_w.reshape(1, M_V), batch=batch, seq=seq)

    y_c = _conv(p_conv, g_conv, conv_w, batch=batch, seq=seq, bn=512)

    merged = _merge(y_m, y_c, y_x, w_proj_m.astype(BF16), w_proj_c.astype(BF16), w_proj_x.astype(BF16),
                    gates, bm=512, bn=512)
    out = _out(merged, w_out.astype(BF16), x2, ln_w.reshape(1, d), ln_b.reshape(1, d),
               bm=512, bn=512, alpha=alpha)
    return out.reshape(batch, seq, d)


def kernel(x, mem, w_in, b_in, conv_w, mh_norm_w, w_mem_kv, w_proj_m, w_proj_c, w_proj_x, w_out, ln_w, ln_b):
    depth = w_in.shape[0]
    alpha = (2 * depth) ** 0.25
    w_in_t = jnp.swapaxes(w_in, 1, 2)
    for l in range(depth):
        x = _layer(x, mem, w_in_t, b_in[l], l, conv_w[l], mh_norm_w[l], w_mem_kv, w_proj_m[l],
                   w_proj_c[l], w_proj_x[l], w_out[l], ln_w[l], ln_b[l], alpha=alpha)
    return x
```

```python
import functools

import jax
import jax.numpy as jnp
from jax import lax
from jax.experimental import pallas as pl
from jax.experimental.pallas import tpu as pltpu

F32 = jnp.float32
BF16 = jnp.bfloat16

D_MODEL = 4096
M_HEADS = 8
M_QK_DIM = 256
M_V_DIM = 512
M_QK = M_HEADS * M_QK_DIM
M_V = M_HEADS * M_V_DIM
CONV_WIDTH = 2048
CONV_K = 3
X_HEADS = 4
X_HEAD_DIM = 512
X_W = X_HEADS * X_HEAD_DIM
N_BRANCH = 3
LN_EPS = 1e-5

OFF_Q = 0
OFF_K = OFF_Q + M_QK
OFF_V = OFF_K + M_QK
OFF_MO = OFF_V + M_V
OFF_MZ = OFF_MO + M_V
OFF_MI = OFF_MZ + M_V
OFF_MF = OFF_MI + M_HEADS
OFF_CB = OFF_MF + M_HEADS
OFF_CC = OFF_CB + CONV_WIDTH
OFF_CX = OFF_CC + CONV_WIDTH
OFF_CZ = OFF_CX + CONV_WIDTH
OFF_XQ = OFF_CZ + CONV_WIDTH
OFF_XZ = OFF_XQ + X_W
OFF_G = OFF_XZ + X_W
D_IN = OFF_G + N_BRANCH * D_MODEL

V7X_LANES = 128
V7X_VMEM_LIMIT = 56 * 1024 * 1024
MLSTM_CHUNK = 256
MLSTM_HEAD_GROUP = 8


def _sigmoid(x):
    return 0.5 * jnp.tanh(0.5 * x) + 0.5


def _silu(x):
    return x * _sigmoid(x)


def _log_sigmoid(x):
    return jnp.minimum(x, 0.0) - jnp.log1p(jnp.exp(-jnp.abs(x)))


def _proj_body(*refs, n_w, has_bias, transposed, epilogue):
    x_ref = refs[0]
    w_refs = refs[1:1 + n_w]
    n_b = n_w if has_bias else 0
    b_refs = refs[1 + n_w:1 + n_w + n_b]
    o_refs = refs[1 + n_w + n_b:]
    x = x_ref[...]
    contract = (((1,), (1,)), ((), ())) if transposed else (((1,), (0,)), ((), ()))
    us = []
    for idx, w in enumerate(w_refs):
        u = lax.dot_general(x, w[...].astype(BF16), contract, preferred_element_type=F32)
        if has_bias:
            u = u + b_refs[idx][...]
        us.append(u)
    for o, val in zip(o_refs, epilogue(*us)):
        o[...] = val.astype(o.dtype)


def _proj(x, w, biases, layer, col_offs, width, epilogue, out_dtypes, *, transposed, bm, bn, name):
    m, k = x.shape
    n_w = len(col_offs)
    in_specs = [pl.BlockSpec((bm, k), lambda i, j: (i, 0))]
    if transposed:
        base = layer * w.shape[1]
        w = w.reshape(-1, k)
        in_specs += [pl.BlockSpec((pl.Element(bn), pl.Element(k)), lambda i, j, o=base + off: (pl.multiple_of(o + j * bn, 8), 0))
                     for off in col_offs]
    else:
        in_specs += [pl.BlockSpec((None, k, bn), lambda i, j, o=off // bn: (layer, 0, o + j))
                     for off in col_offs]
    operands = [x] + [w] * n_w
    if biases is not None:
        in_specs += [pl.BlockSpec((1, bn), lambda i, j: (0, j)) for _ in biases]
        operands += list(biases)
    out_specs = [pl.BlockSpec((bm, bn), lambda i, j: (i, j)) for _ in out_dtypes]
    out_shape = [jax.ShapeDtypeStruct((m, width), dt) for dt in out_dtypes]
    return pl.pallas_call(
        functools.partial(_proj_body, n_w=n_w, has_bias=biases is not None, transposed=transposed,
                          epilogue=epilogue),
        grid=(m // bm, width // bn), in_specs=in_specs, out_specs=out_specs, out_shape=out_shape,
        compiler_params=pltpu.CompilerParams(
            dimension_semantics=("parallel", "arbitrary"), vmem_limit_bytes=V7X_VMEM_LIMIT),
        name=name,
    )(*operands)


def _xcast_if_body(x_ref, w_ref, b_ref, xb_ref, if_ref):
    xb = x_ref[...].astype(BF16)
    xb_ref[...] = xb
    if_ref[...] = lax.dot_general(xb, w_ref[...].astype(BF16), (((1,), (1,)), ((), ())),
                                  preferred_element_type=F32) + b_ref[...]


def _xcast_if(x2, w_t, bias, layer, *, bm):
    m, k = x2.shape
    row0 = layer * w_t.shape[1] + OFF_MI
    return pl.pallas_call(
        _xcast_if_body,
        grid=(m // bm,),
        in_specs=[pl.BlockSpec((bm, k), lambda i: (i, 0)),
                  pl.BlockSpec((pl.Element(V7X_LANES), pl.Element(k)), lambda i: (row0, 0)),
                  pl.BlockSpec((1, V7X_LANES), lambda i: (0, 0))],
        out_specs=[pl.BlockSpec((bm, k), lambda i: (i, 0)),
                   pl.BlockSpec((bm, V7X_LANES), lambda i: (i, 0))],
        out_shape=[jax.ShapeDtypeStruct((m, k), BF16), jax.ShapeDtypeStruct((m, V7X_LANES), F32)],
        compiler_params=pltpu.CompilerParams(
            dimension_semantics=("parallel",), vmem_limit_bytes=V7X_VMEM_LIMIT),
        name="xcast_if",
    )(x2, w_t.reshape(-1, k), bias)


def _xattn_body(q_ref, k_ref, v_ref, z_ref, o_ref):
    scale = X_HEAD_DIM ** -0.5
    for h in range(X_HEADS):
        sl = slice(h * X_HEAD_DIM, (h + 1) * X_HEAD_DIM)
        s = lax.dot_general(q_ref[:, sl], k_ref[:, sl], (((1,), (1,)), ((), ())),
                            preferred_element_type=F32) * scale
        e = jnp.exp(s - jnp.max(s, axis=-1, keepdims=True))
        p = e / jnp.sum(e, axis=-1, keepdims=True)
        a = jnp.dot(p.astype(BF16), v_ref[:, sl], preferred_element_type=F32)
        o_ref[:, sl] = (a * z_ref[:, sl].astype(F32)).astype(o_ref.dtype)


def _xattn(xq, sxz, kv, *, batch, seq, mem_len, bs):
    nb = seq // bs
    return pl.pallas_call(
        _xattn_body,
        grid=(batch, nb),
        in_specs=[pl.BlockSpec((bs, X_W), lambda b, i: (b * nb + i, 0)),
                  pl.BlockSpec((mem_len, X_W), lambda b, i: (b, 0)),
                  pl.BlockSpec((mem_len, X_W), lambda b, i: (b, 1)),
                  pl.BlockSpec((bs, X_W), lambda b, i: (b * nb + i, 0))],
        out_specs=pl.BlockSpec((bs, X_W), lambda b, i: (b * nb + i, 0)),
        out_shape=jax.ShapeDtypeStruct((batch * seq, X_W), BF16),
        compiler_params=pltpu.CompilerParams(
            dimension_semantics=("parallel", "arbitrary"), vmem_limit_bytes=V7X_VMEM_LIMIT),
        name="xattn",
    )(xq, kv, kv, sxz)


def _mlstm_body(q_ref, k_ref, v_ref, g_ref, gate_ref, nw_ref, o_ref, c_ref, n_ref, m_ref):
    @pl.when(pl.program_id(2) == 0)
    def _():
        c_ref[...] = jnp.zeros_like(c_ref)
        n_ref[...] = jnp.zeros_like(n_ref)
        m_ref[...] = jnp.zeros_like(m_ref)

    for hg in range(MLSTM_HEAD_GROUP):
        qk_cols = slice(hg * M_QK_DIM, (hg + 1) * M_QK_DIM)
        v_cols = slice(hg * M_V_DIM, (hg + 1) * M_V_DIM)
        _mlstm_head(pl.program_id(1) * MLSTM_HEAD_GROUP + hg,
                    q_ref.at[:, qk_cols], k_ref.at[:, qk_cols], v_ref.at[:, v_cols], g_ref,
                    gate_ref.at[:, v_cols], nw_ref.at[:, v_cols], o_ref.at[:, v_cols],
                    c_ref.at[hg], n_ref.at[hg], m_ref.at[hg])


def _mlstm_head(h, q_ref, k_ref, v_ref, g_ref, gate_ref, nw_ref, o_ref, c_ref, n_ref, m_ref):
    L = MLSTM_CHUNK
    scale = M_QK_DIM ** -0.5
    q = q_ref[...]
    k = k_ref[...]
    v = v_ref[...]
    ig = g_ref[pl.ds(h, 1), :]
    lf = _log_sigmoid(g_ref[pl.ds(M_HEADS + h, 1), :])

    t_idx = lax.broadcasted_iota(jnp.int32, (L, L), 0)
    j_idx = lax.broadcasted_iota(jnp.int32, (L, L), 1)
    causal = j_idx <= t_idx
    eye = j_idx == t_idx
    b_col = jnp.sum(jnp.where(causal, lf, 0.0), axis=1, keepdims=True)
    b_row = jnp.sum(jnp.where(eye, b_col, 0.0), axis=0, keepdims=True)
    i_col = jnp.sum(jnp.where(eye, ig, 0.0), axis=1, keepdims=True)
    g = jnp.sum(lf, axis=1, keepdims=True)
    m_prev = m_ref[:, 0:1]

    log_d = jnp.where(causal, b_col - b_row + ig, -jnp.inf)
    m_inter = b_col + m_prev
    m_t = jnp.maximum(jnp.max(log_d, axis=1, keepdims=True), m_inter)
    s = lax.dot_general(q, k, (((1,), (1,)), ((), ())), preferred_element_type=F32)
    s = s * scale * jnp.exp(log_d - m_t)
    inter = jnp.exp(m_inter - m_t) * scale
    c_st = c_ref[...]
    num = (jnp.dot(s.astype(BF16), v, preferred_element_type=F32)
           + inter * jnp.dot(q, c_st.astype(BF16), preferred_element_type=F32))
    qn = jnp.sum(q.astype(F32) * n_ref[...], axis=1, keepdims=True)
    den = jnp.sum(s, axis=1, keepdims=True) + inter * qn
    hh = num / jnp.maximum(jnp.abs(den), jnp.exp(-m_t))

    w_col = g - b_col + i_col
    m_new = jnp.maximum(g + m_prev, jnp.max(w_col, axis=0, keepdims=True))
    decay = jnp.exp(g + m_prev - m_new)
    wk = jnp.exp(w_col - m_new) * k.astype(F32)
    c_ref[...] = decay * c_st + lax.dot_general(
        wk.astype(BF16), v, (((0,), (0,)), ((), ())), preferred_element_type=F32)
    n_ref[...] = decay * n_ref[...] + jnp.sum(wk, axis=0, keepdims=True)
    m_ref[...] = jnp.broadcast_to(m_new, m_ref.shape)

    mu = jnp.mean(hh, axis=1, keepdims=True)
    hc = hh - mu
    var = jnp.mean(hc * hc, axis=1, keepdims=True)
    y = hc * lax.rsqrt(var + LN_EPS) * nw_ref[...] * gate_ref[...].astype(F32)
    o_ref[...] = y.astype(o_ref.dtype)


def _mlstm(qkv, gates_t, gate_m, norm_w, *, batch, seq):
    L = MLSTM_CHUNK
    nc = seq // L
    G = MLSTM_HEAD_GROUP
    qk_w = G * M_QK_DIM
    v_w = G * M_V_DIM
    kq = M_QK // qk_w
    kv = (2 * M_QK) // v_w
    return pl.pallas_call(
        _mlstm_body,
        grid=(batch, M_HEADS // G, nc),
        in_specs=[pl.BlockSpec((L, qk_w), lambda b, h, c: (b * nc + c, h)),
                  pl.BlockSpec((L, qk_w), lambda b, h, c: (b * nc + c, kq + h)),
                  pl.BlockSpec((L, v_w), lambda b, h, c: (b * nc + c, kv + h)),
                  pl.BlockSpec((None, 2 * M_HEADS, L), lambda b, h, c: (b, 0, c)),
                  pl.BlockSpec((L, v_w), lambda b, h, c: (b * nc + c, h)),
                  pl.BlockSpec((1, v_w), lambda b, h, c: (0, h))],
        out_specs=pl.BlockSpec((L, v_w), lambda b, h, c: (b * nc + c, h)),
        out_shape=jax.ShapeDtypeStruct((batch * seq, M_V), BF16),
        scratch_shapes=[pltpu.VMEM((G, M_QK_DIM, M_V_DIM), F32),
                        pltpu.VMEM((G, 1, M_QK_DIM), F32),
                        pltpu.VMEM((G, 1, V7X_LANES), F32)],
        compiler_params=pltpu.CompilerParams(
            dimension_semantics=("parallel", "parallel", "arbitrary"),
            vmem_limit_bytes=V7X_VMEM_LIMIT),
        name="mlstm",
    )(qkv, qkv, qkv, gates_t, gate_m, norm_w)


def _conv_body(p_ref, gc_ref, w_ref, o_ref):
    p = p_ref[...].astype(F32)
    row = lax.broadcasted_iota(jnp.int32, p.shape, 0)
    acc = p * w_ref[CONV_K - 1:CONV_K, :]
    for d in range(1, CONV_K):
        shifted = jnp.where(row >= d, pltpu.roll(p, d, 0), 0.0)
        acc = acc + shifted * w_ref[CONV_K - 1 - d:CONV_K - d, :]
    o_ref[...] = (acc * gc_ref[...].astype(F32)).astype(o_ref.dtype)


def _conv(p, gc, conv_w, *, batch, seq, bn):
    return pl.pallas_call(
        _conv_body,
        grid=(batch, CONV_WIDTH // bn),
        in_specs=[pl.BlockSpec((seq, bn), lambda b, j: (b, j)),
                  pl.BlockSpec((seq, bn), lambda b, j: (b, j)),
                  pl.BlockSpec((CONV_K, bn), lambda b, j: (0, j))],
        out_specs=pl.BlockSpec((seq, bn), lambda b, j: (b, j)),
        out_shape=jax.ShapeDtypeStruct((batch * seq, CONV_WIDTH), BF16),
        compiler_params=pltpu.CompilerParams(
            dimension_semantics=("parallel", "arbitrary"), vmem_limit_bytes=V7X_VMEM_LIMIT),
        name="conv",
    )(p, gc, conv_w)


def _merge_body(ym_ref, yc_ref, yx_ref, wm_ref, wc_ref, wx_ref, gm_ref, gc_ref, gx_ref, o_ref):
    acc = gm_ref[...].astype(F32) * jnp.dot(ym_ref[...], wm_ref[...], preferred_element_type=F32)
    acc = acc + gc_ref[...].astype(F32) * jnp.dot(yc_ref[...], wc_ref[...], preferred_element_type=F32)
    acc = acc + gx_ref[...].astype(F32) * jnp.dot(yx_ref[...], wx_ref[...], preferred_element_type=F32)
    o_ref[...] = acc.astype(o_ref.dtype)


def _merge(y_m, y_c, y_x, wm, wc, wx, gates, *, bm, bn):
    m = y_m.shape[0]
    nb = D_MODEL // bn
    return pl.pallas_call(
        _merge_body,
        grid=(m // bm, nb),
        in_specs=[pl.BlockSpec((bm, M_V), lambda i, j: (i, 0)),
                  pl.BlockSpec((bm, CONV_WIDTH), lambda i, j: (i, 0)),
                  pl.BlockSpec((bm, X_W), lambda i, j: (i, 0)),
                  pl.BlockSpec((M_V, bn), lambda i, j: (0, j)),
                  pl.BlockSpec((CONV_WIDTH, bn), lambda i, j: (0, j)),
                  pl.BlockSpec((X_W, bn), lambda i, j: (0, j)),
                  pl.BlockSpec((bm, bn), lambda i, j: (i, j)),
                  pl.BlockSpec((bm, bn), lambda i, j: (i, nb + j)),
                  pl.BlockSpec((bm, bn), lambda i, j: (i, 2 * nb + j))],
        out_specs=pl.BlockSpec((bm, bn), lambda i, j: (i, j)),
        out_shape=jax.ShapeDtypeStruct((m, D_MODEL), BF16),
        compiler_params=pltpu.CompilerParams(
            dimension_semantics=("parallel", "arbitrary"), vmem_limit_bytes=V7X_VMEM_LIMIT),
        name="merge",
    )(y_m, y_c, y_x, wm, wc, wx, gates, gates, gates)


def _out_body(mg_ref, w_ref, x_ref, lw_ref, lb_ref, o_ref, *, bn, alpha):
    j = pl.program_id(1)
    z = alpha * x_ref[...] + jnp.dot(mg_ref[...], w_ref[...], preferred_element_type=F32)
    o_ref[:, pl.ds(pl.multiple_of(j * bn, bn), bn)] = z

    @pl.when(j == pl.num_programs(1) - 1)
    def _():
        zz = o_ref[...]
        mu = jnp.mean(zz, axis=1, keepdims=True)
        zc = zz - mu
        var = jnp.mean(zc * zc, axis=1, keepdims=True)
        o_ref[...] = zc * lax.rsqrt(var + LN_EPS) * lw_ref[...] + lb_ref[...]


def _out(merged, w_out, x2, ln_w, ln_b, *, bm, bn, alpha):
    m = merged.shape[0]
    return pl.pallas_call(
        functools.partial(_out_body, bn=bn, alpha=alpha),
        grid=(m // bm, D_MODEL // bn),
        in_specs=[pl.BlockSpec((bm, D_MODEL), lambda i, j: (i, 0)),
                  pl.BlockSpec((D_MODEL, bn), lambda i, j: (0, j)),
                  pl.BlockSpec((bm, bn), lambda i, j: (i, j)),
                  pl.BlockSpec((1, D_MODEL), lambda i, j: (0, 0)),
                  pl.BlockSpec((1, D_MODEL), lambda i, j: (0, 0))],
        out_specs=pl.BlockSpec((bm, D_MODEL), lambda i, j: (i, 0)),
        out_shape=jax.ShapeDtypeStruct((m, D_MODEL), F32),
        compiler_params=pltpu.CompilerParams(
            dimension_semantics=("parallel", "arbitrary"), vmem_limit_bytes=V7X_VMEM_LIMIT),
        name="outproj_ln",
    )(merged, w_out, x2, ln_w, ln_b)


def _layer(x, mem, w_in_t, b_in, layer, conv_w, mh_norm_w, w_mem_kv, w_proj_m, w_proj_c, w_proj_x, w_out,
           ln_w, ln_b, *, alpha):
    batch, seq, d = x.shape
    mem_len = mem.shape[1]
    tokens = batch * seq
    x2 = x.reshape(tokens, d)

    def bseg(off, width):
        return b_in[off:off + width].reshape(1, width)

    xb, if_pre = _xcast_if(x2, w_in_t, bseg(OFF_MI, V7X_LANES), layer, bm=512)

    def proj(offs, width, epilogue, out_dtypes, bn, name):
        return _proj(xb, w_in_t, [bseg(o, width) for o in offs], layer, offs, width, epilogue, out_dtypes,
                     transposed=True, bm=1024, bn=bn, name=name)

    ident = lambda u: (u,)
    (qkv,) = proj([OFF_Q], 2 * M_QK + M_V, ident, [BF16], 512, "proj_qkv")
    (gate_m,) = proj([OFF_MO, OFF_MZ], M_V, lambda mo, mz: (_sigmoid(mo) * _silu(mz),), [BF16], 256,
                     "proj_mgate")
    (p_conv,) = proj([OFF_CC, OFF_CX], CONV_WIDTH, lambda cc, cx: (cc * cx,), [BF16], 256, "proj_conv_p")
    (g_conv,) = proj([OFF_CB, OFF_CZ], CONV_WIDTH, lambda cb, cz: (cb * _silu(cz),), [BF16], 256,
                     "proj_conv_g")
    xq, sxz = proj([OFF_XQ, OFF_XZ], X_W, lambda q, z: (q, _silu(z)), [BF16, BF16], 256, "proj_xattn")
    (gates,) = proj([OFF_G], N_BRANCH * D_MODEL, lambda u: (_sigmoid(u),), [BF16], 512, "proj_gates")

    mem2 = mem.reshape(batch * mem_len, d).astype(BF16)
    (kv,) = _proj(mem2, w_mem_kv, None, layer, [0], 2 * X_W, ident, [BF16], transposed=False,
                  bm=batch * mem_len, bn=512, name="proj_memkv")
    y_x = _xattn(xq, sxz, kv, batch=batch, seq=seq, mem_len=mem_len, bs=512)

    n_if = 2 * M_HEADS
    gates_t = if_pre[:, :n_if].reshape(batch, seq, n_if).transpose(0, 2, 1)
    y_m = _mlstm(qkv, gates_t, gate_m, mh_norm_w.reshape(1, M_V), batch=batch, seq=seq)

    y_c = _conv(p_conv, g_conv, conv_w, batch=batch, seq=seq, bn=512)

    merged = _merge(y_m, y_c, y_x, w_proj_m.astype(BF16), w_proj_c.astype(BF16), w_proj_x.astype(BF16),
                    gates, bm=512, bn=512)
    out = _out(merged, w_out.astype(BF16), x2, ln_w.reshape(1, d), ln_b.reshape(1, d),
               bm=512, bn=512, alpha=alpha)
    return out.reshape(batch, seq, d)


def kernel(x, mem, w_in, b_in, conv_w, mh_norm_w, w_mem_kv, w_proj_m, w_proj_c, w_proj_x, w_out, ln_w, ln_b):
    depth = w_in.shape[0]
    alpha = (2 * depth) ** 0.25
    w_in_t = jnp.swapaxes(w_in, 1, 2)
    for l in range(depth):
        x = _layer(x, mem, w_in_t, b_in[l], l, conv_w[l], mh_norm_w[l], w_mem_kv, w_proj_m[l],
                   w_proj_c[l], w_proj_x[l], w_out[l], ln_w[l], ln_b[l], alpha=alpha)
    return x
```

```python
import functools

import jax
import jax.numpy as jnp
from jax import lax
from jax.experimental import pallas as pl
from jax.experimental.pallas import tpu as pltpu

F32 = jnp.float32
BF16 = jnp.bfloat16

D_MODEL = 4096
M_HEADS = 8
M_QK_DIM = 256
M_V_DIM = 512
M_QK = M_HEADS * M_QK_DIM
M_V = M_HEADS * M_V_DIM
CONV_WIDTH = 2048
CONV_K = 3
X_HEADS = 4
X_HEAD_DIM = 512
X_W = X_HEADS * X_HEAD_DIM
N_BRANCH = 3
LN_EPS = 1e-5

OFF_Q = 0
OFF_K = OFF_Q + M_QK
OFF_V = OFF_K + M_QK
OFF_MO = OFF_V + M_V
OFF_MZ = OFF_MO + M_V
OFF_MI = OFF_MZ + M_V
OFF_MF = OFF_MI + M_HEADS
OFF_CB = OFF_MF + M_HEADS
OFF_CC = OFF_CB + CONV_WIDTH
OFF_CX = OFF_CC + CONV_WIDTH
OFF_CZ = OFF_CX + CONV_WIDTH
OFF_XQ = OFF_CZ + CONV_WIDTH
OFF_XZ = OFF_XQ + X_W
OFF_G = OFF_XZ + X_W
D_IN = OFF_G + N_BRANCH * D_MODEL

V7X_LANES = 128
V7X_VMEM_LIMIT = 56 * 1024 * 1024
MLSTM_CHUNK = 256
CAST_ROWS = 64
MLSTM_HEAD_GROUP = 8


def _sigmoid(x):
    return 0.5 * jnp.tanh(0.5 * x) + 0.5


def _silu(x):
    return x * _sigmoid(x)


def _log_sigmoid(x):
    return jnp.minimum(x, 0.0) - jnp.log1p(jnp.exp(-jnp.abs(x)))


def _proj_body(*refs, n_w, has_bias, transposed, epilogue, cast_runs):
    n_cast = len(cast_runs)
    x_ref = refs[0]
    w_refs = refs[1:1 + n_w]
    n_b = n_w if has_bias else 0
    b_refs = refs[1 + n_w:1 + n_w + n_b]
    n_in = 1 + n_w + n_b
    cast_in = refs[n_in:n_in + n_cast]
    o_refs = refs[n_in + n_cast:len(refs) - n_cast]
    cast_out = refs[len(refs) - n_cast:]
    step = pl.program_id(0) * pl.num_programs(1) + pl.program_id(1)
    for src, dst, (first, count) in zip(cast_in, cast_out, cast_runs):
        @pl.when((step >= first) & (step < first + count))
        def _(src=src, dst=dst):
            dst[...] = src[...].astype(dst.dtype)

    x = x_ref[...]
    contract = (((1,), (1,)), ((), ())) if transposed else (((1,), (0,)), ((), ()))
    us = []
    for idx, w in enumerate(w_refs):
        u = lax.dot_general(x, w[...].astype(BF16), contract, preferred_element_type=F32)
        if has_bias:
            u = u + b_refs[idx][...]
        us.append(u)
    for o, val in zip(o_refs, epilogue(*us)):
        o[...] = val.astype(o.dtype)


def _proj(x, w, biases, layer, col_offs, width, epilogue, out_dtypes, *, transposed, bm, bn, name,
          casts=()):
    m, k = x.shape
    n_w = len(col_offs)
    n_j = width // bn
    in_specs = [pl.BlockSpec((bm, k), lambda i, j: (i, 0))]
    if transposed:
        base = layer * w.shape[1]
        w = w.reshape(-1, k)
        in_specs += [pl.BlockSpec((pl.Element(bn), pl.Element(k)), lambda i, j, o=base + off: (pl.multiple_of(o + j * bn, 8), 0))
                     for off in col_offs]
    else:
        in_specs += [pl.BlockSpec((None, k, bn), lambda i, j, o=off // bn: (layer, 0, o + j))
                     for off in col_offs]
    operands = [x] + [w] * n_w
    if biases is not None:
        in_specs += [pl.BlockSpec((1, bn), lambda i, j: (0, j)) for _ in biases]
        operands += list(biases)
    out_specs = [pl.BlockSpec((bm, bn), lambda i, j: (i, j)) for _ in out_dtypes]
    out_shape = [jax.ShapeDtypeStruct((m, width), dt) for dt in out_dtypes]
    first_step = 0
    cast_runs = []
    for arr in casts:
        _, rows, cols = arr.shape
        n_slabs = rows // CAST_ROWS
        cast_runs.append((first_step, n_slabs))

        def slab(i, j, s=first_step, n=n_slabs):
            return jnp.clip(i * n_j + j - s, 0, n - 1)

        in_specs.append(pl.BlockSpec((None, CAST_ROWS, cols), lambda i, j, f=slab: (layer, f(i, j), 0)))
        out_specs.append(pl.BlockSpec((CAST_ROWS, cols), lambda i, j, f=slab: (f(i, j), 0)))
        out_shape.append(jax.ShapeDtypeStruct((rows, cols), BF16))
        operands.append(arr)
        first_step += n_slabs
    assert first_step <= (m // bm) * n_j, "not enough grid steps to walk the cast arrays"
    return pl.pallas_call(
        functools.partial(_proj_body, n_w=n_w, has_bias=biases is not None, transposed=transposed,
                          epilogue=epilogue, cast_runs=tuple(cast_runs)),
        grid=(m // bm, n_j), in_specs=in_specs, out_specs=out_specs, out_shape=out_shape,
        compiler_params=pltpu.CompilerParams(
            dimension_semantics=("parallel", "arbitrary"), vmem_limit_bytes=V7X_VMEM_LIMIT),
        name=name,
    )(*operands)


def _xcast_if_body(x_ref, w_ref, b_ref, xb_ref, if_ref):
    xb = x_ref[...].astype(BF16)
    xb_ref[...] = xb
    if_ref[...] = lax.dot_general(xb, w_ref[...].astype(BF16), (((1,), (1,)), ((), ())),
                                  preferred_element_type=F32) + b_ref[...]


def _xcast_if(x2, w_t, bias, layer, *, bm):
    m, k = x2.shape
    row0 = layer * w_t.shape[1] + OFF_MI
    return pl.pallas_call(
        _xcast_if_body,
        grid=(m // bm,),
        in_specs=[pl.BlockSpec((bm, k), lambda i: (i, 0)),
                  pl.BlockSpec((pl.Element(V7X_LANES), pl.Element(k)), lambda i: (row0, 0)),
                  pl.BlockSpec((1, V7X_LANES), lambda i: (0, 0))],
        out_specs=[pl.BlockSpec((bm, k), lambda i: (i, 0)),
                   pl.BlockSpec((bm, V7X_LANES), lambda i: (i, 0))],
        out_shape=[jax.ShapeDtypeStruct((m, k), BF16), jax.ShapeDtypeStruct((m, V7X_LANES), F32)],
        compiler_params=pltpu.CompilerParams(
            dimension_semantics=("parallel",), vmem_limit_bytes=V7X_VMEM_LIMIT),
        name="xcast_if",
    )(x2, w_t.reshape(-1, k), bias)


def _xattn_body(q_ref, k_ref, v_ref, z_ref, o_ref):
    scale = X_HEAD_DIM ** -0.5
    for h in range(X_HEADS):
        sl = slice(h * X_HEAD_DIM, (h + 1) * X_HEAD_DIM)
        s = lax.dot_general(q_ref[:, sl], k_ref[:, sl], (((1,), (1,)), ((), ())),
                            preferred_element_type=F32) * scale
        e = jnp.exp(s - jnp.max(s, axis=-1, keepdims=True))
        p = e / jnp.sum(e, axis=-1, keepdims=True)
        a = jnp.dot(p.astype(BF16), v_ref[:, sl], preferred_element_type=F32)
        o_ref[:, sl] = (a * z_ref[:, sl].astype(F32)).astype(o_ref.dtype)


def _xattn(xq, sxz, kv, *, batch, seq, mem_len, bs):
    nb = seq // bs
    return pl.pallas_call(
        _xattn_body,
        grid=(batch, nb),
        in_specs=[pl.BlockSpec((bs, X_W), lambda b, i: (b * nb + i, 0)),
                  pl.BlockSpec((mem_len, X_W), lambda b, i: (b, 0)),
                  pl.BlockSpec((mem_len, X_W), lambda b, i: (b, 1)),
                  pl.BlockSpec((bs, X_W), lambda b, i: (b * nb + i, 0))],
        out_specs=pl.BlockSpec((bs, X_W), lambda b, i: (b * nb + i, 0)),
        out_shape=jax.ShapeDtypeStruct((batch * seq, X_W), BF16),
        compiler_params=pltpu.CompilerParams(
            dimension_semantics=("parallel", "arbitrary"), vmem_limit_bytes=V7X_VMEM_LIMIT),
        name="xattn",
    )(xq, kv, kv, sxz)


def _mlstm_body(q_ref, k_ref, v_ref, g_ref, gate_ref, nw_ref, o_ref, c_ref, n_ref, m_ref):
    @pl.when(pl.program_id(2) == 0)
    def _():
        c_ref[...] = jnp.zeros_like(c_ref)
        n_ref[...] = jnp.zeros_like(n_ref)
        m_ref[...] = jnp.zeros_like(m_ref)

    for hg in range(MLSTM_HEAD_GROUP):
        qk_cols = slice(hg * M_QK_DIM, (hg + 1) * M_QK_DIM)
        v_cols = slice(hg * M_V_DIM, (hg + 1) * M_V_DIM)
        _mlstm_head(pl.program_id(1) * MLSTM_HEAD_GROUP + hg,
                    q_ref.at[:, qk_cols], k_ref.at[:, qk_cols], v_ref.at[:, v_cols], g_ref,
                    gate_ref.at[:, v_cols], nw_ref.at[:, v_cols], o_ref.at[:, v_cols],
                    c_ref.at[hg], n_ref.at[hg], m_ref.at[hg])


def _mlstm_head(h, q_ref, k_ref, v_ref, g_ref, gate_ref, nw_ref, o_ref, c_ref, n_ref, m_ref):
    L = MLSTM_CHUNK
    scale = M_QK_DIM ** -0.5
    q = q_ref[...]
    k = k_ref[...]
    v = v_ref[...]
    ig = g_ref[pl.ds(h, 1), :]
    lf = _log_sigmoid(g_ref[pl.ds(M_HEADS + h, 1), :])

    t_idx = lax.broadcasted_iota(jnp.int32, (L, L), 0)
    j_idx = lax.broadcasted_iota(jnp.int32, (L, L), 1)
    causal = j_idx <= t_idx
    eye = j_idx == t_idx
    b_col = jnp.sum(jnp.where(causal, lf, 0.0), axis=1, keepdims=True)
    b_row = jnp.sum(jnp.where(eye, b_col, 0.0), axis=0, keepdims=True)
    i_col = jnp.sum(jnp.where(eye, ig, 0.0), axis=1, keepdims=True)
    g = jnp.sum(lf, axis=1, keepdims=True)
    m_prev = m_ref[:, 0:1]

    log_d = jnp.where(causal, b_col - b_row + ig, -jnp.inf)
    m_inter = b_col + m_prev
    m_t = jnp.maximum(jnp.max(log_d, axis=1, keepdims=True), m_inter)
    s = lax.dot_general(q, k, (((1,), (1,)), ((), ())), preferred_element_type=F32)
    s = s * scale * jnp.exp(log_d - m_t)
    inter = jnp.exp(m_inter - m_t) * scale
    c_st = c_ref[...]
    num = (jnp.dot(s.astype(BF16), v, preferred_element_type=F32)
           + inter * jnp.dot(q, c_st.astype(BF16), preferred_element_type=F32))
    qn = jnp.sum(q.astype(F32) * n_ref[...], axis=1, keepdims=True)
    den = jnp.sum(s, axis=1, keepdims=True) + inter * qn
    hh = num / jnp.maximum(jnp.abs(den), jnp.exp(-m_t))

    w_col = g - b_col + i_col
    m_new = jnp.maximum(g + m_prev, jnp.max(w_col, axis=0, keepdims=True))
    decay = jnp.exp(g + m_prev - m_new)
    wk = jnp.exp(w_col - m_new) * k.astype(F32)
    c_ref[...] = decay * c_st + lax.dot_general(
        wk.astype(BF16), v, (((0,), (0,)), ((), ())), preferred_element_type=F32)
    n_ref[...] = decay * n_ref[...] + jnp.sum(wk, axis=0, keepdims=True)
    m_ref[...] = jnp.broadcast_to(m_new, m_ref.shape)

    mu = jnp.mean(hh, axis=1, keepdims=True)
    hc = hh - mu
    var = jnp.mean(hc * hc, axis=1, keepdims=True)
    y = hc * lax.rsqrt(var + LN_EPS) * nw_ref[...] * gate_ref[...].astype(F32)
    o_ref[...] = y.astype(o_ref.dtype)


def _mlstm(qkv, gates_t, gate_m, norm_w, *, batch, seq):
    L = MLSTM_CHUNK
    nc = seq // L
    G = MLSTM_HEAD_GROUP
    qk_w = G * M_QK_DIM
    v_w = G * M_V_DIM
    kq = M_QK // qk_w
    kv = (2 * M_QK) // v_w
    return pl.pallas_call(
        _mlstm_body,
        grid=(batch, M_HEADS // G, nc),
        in_specs=[pl.BlockSpec((L, qk_w), lambda b, h, c: (b * nc + c, h)),
                  pl.BlockSpec((L, qk_w), lambda b, h, c: (b * nc + c, kq + h)),
                  pl.BlockSpec((L, v_w), lambda b, h, c: (b * nc + c, kv + h)),
                  pl.BlockSpec((None, 2 * M_HEADS, L), lambda b, h, c: (b, 0, c)),
                  pl.BlockSpec((L, v_w), lambda b, h, c: (b * nc + c, h)),
                  pl.BlockSpec((1, v_w), lambda b, h, c: (0, h))],
        out_specs=pl.BlockSpec((L, v_w), lambda b, h, c: (b * nc + c, h)),
        out_shape=jax.ShapeDtypeStruct((batch * seq, M_V), BF16),
        scratch_shapes=[pltpu.VMEM((G, M_QK_DIM, M_V_DIM), F32),
                        pltpu.VMEM((G, 1, M_QK_DIM), F32),
                        pltpu.VMEM((G, 1, V7X_LANES), F32)],
        compiler_params=pltpu.CompilerParams(
            dimension_semantics=("parallel", "parallel", "arbitrary"),
            vmem_limit_bytes=V7X_VMEM_LIMIT),
        name="mlstm",
    )(qkv, qkv, qkv, gates_t, gate_m, norm_w)


def _conv_body(p_ref, gc_ref, w_ref, o_ref):
    p = p_ref[...].astype(F32)
    row = lax.broadcasted_iota(jnp.int32, p.shape, 0)
    acc = p * w_ref[CONV_K - 1:CONV_K, :]
    for d in range(1, CONV_K):
        shifted = jnp.where(row >= d, pltpu.roll(p, d, 0), 0.0)
        acc = acc + shifted * w_ref[CONV_K - 1 - d:CONV_K - d, :]
    o_ref[...] = (acc * gc_ref[...].astype(F32)).astype(o_ref.dtype)


def _conv(p, gc, conv_w, *, batch, seq, bn):
    return pl.pallas_call(
        _conv_body,
        grid=(batch, CONV_WIDTH // bn),
        in_specs=[pl.BlockSpec((seq, bn), lambda b, j: (b, j)),
                  pl.BlockSpec((seq, bn), lambda b, j: (b, j)),
                  pl.BlockSpec((CONV_K, bn), lambda b, j: (0, j))],
        out_specs=pl.BlockSpec((seq, bn), lambda b, j: (b, j)),
        out_shape=jax.ShapeDtypeStruct((batch * seq, CONV_WIDTH), BF16),
        compiler_params=pltpu.CompilerParams(
            dimension_semantics=("parallel", "arbitrary"), vmem_limit_bytes=V7X_VMEM_LIMIT),
        name="conv",
    )(p, gc, conv_w)


def _merge_body(ym_ref, yc_ref, yx_ref, wm_ref, wc_ref, wx_ref, gm_ref, gc_ref, gx_ref, o_ref):
    acc = gm_ref[...].astype(F32) * jnp.dot(ym_ref[...], wm_ref[...], preferred_element_type=F32)
    acc = acc + gc_ref[...].astype(F32) * jnp.dot(yc_ref[...], wc_ref[...], preferred_element_type=F32)
    acc = acc + gx_ref[...].astype(F32) * jnp.dot(yx_ref[...], wx_ref[...], preferred_element_type=F32)
    o_ref[...] = acc.astype(o_ref.dtype)


def _merge(y_m, y_c, y_x, wm, wc, wx, gates, *, bm, bn):
    m = y_m.shape[0]
    nb = D_MODEL // bn
    return pl.pallas_call(
        _merge_body,
        grid=(m // bm, nb),
        in_specs=[pl.BlockSpec((bm, M_V), lambda i, j: (i, 0)),
                  pl.BlockSpec((bm, CONV_WIDTH), lambda i, j: (i, 0)),
                  pl.BlockSpec((bm, X_W), lambda i, j: (i, 0)),
                  pl.BlockSpec((M_V, bn), lambda i, j: (0, j)),
                  pl.BlockSpec((CONV_WIDTH, bn), lambda i, j: (0, j)),
                  pl.BlockSpec((X_W, bn), lambda i, j: (0, j)),
                  pl.BlockSpec((bm, bn), lambda i, j: (i, j)),
                  pl.BlockSpec((bm, bn), lambda i, j: (i, nb + j)),
                  pl.BlockSpec((bm, bn), lambda i, j: (i, 2 * nb + j))],
        out_specs=pl.BlockSpec((bm, bn), lambda i, j: (i, j)),
        out_shape=jax.ShapeDtypeStruct((m, D_MODEL), BF16),
        compiler_params=pltpu.CompilerParams(
            dimension_semantics=("parallel", "arbitrary"), vmem_limit_bytes=V7X_VMEM_LIMIT),
        name="merge",
    )(y_m, y_c, y_x, wm, wc, wx, gates, gates, gates)


def _out_body(mg_ref, w_ref, x_ref, lw_ref, lb_ref, o_ref, *, bn, alpha):
    j = pl.program_id(1)
    z = alpha * x_ref[...] + jnp.dot(mg_ref[...], w_ref[...], preferred_element_type=F32)
    o_ref[:, pl.ds(pl.multiple_of(j * bn, bn), bn)] = z

    @pl.when(j == pl.num_programs(1) - 1)
    def _():
        zz = o_ref[...]
        mu = jnp.mean(zz, axis=1, keepdims=True)
        zc = zz - mu
        var = jnp.mean(zc * zc, axis=1, keepdims=True)
        o_ref[...] = zc * lax.rsqrt(var + LN_EPS) * lw_ref[...] + lb_ref[...]


def _out(merged, w_out, x2, ln_w, ln_b, *, bm, bn, alpha):
    m = merged.shape[0]
    return pl.pallas_call(
        functools.partial(_out_body, bn=bn, alpha=alpha),
        grid=(m // bm, D_MODEL // bn),
        in_specs=[pl.BlockSpec((bm, D_MODEL), lambda i, j: (i, 0)),
                  pl.BlockSpec((D_MODEL, bn), lambda i, j: (0, j)),
                  pl.BlockSpec((bm, bn), lambda i, j: (i, j)),
                  pl.BlockSpec((1, D_MODEL), lambda i, j: (0, 0)),
                  pl.BlockSpec((1, D_MODEL), lambda i, j: (0, 0))],
        out_specs=pl.BlockSpec((bm, D_MODEL), lambda i, j: (i, 0)),
        out_shape=jax.ShapeDtypeStruct((m, D_MODEL), F32),
        compiler_params=pltpu.CompilerParams(
            dimension_semantics=("parallel", "arbitrary"), vmem_limit_bytes=V7X_VMEM_LIMIT),
        name="outproj_ln",
    )(merged, w_out, x2, ln_w, ln_b)


def _layer(x, mem, w_in_t, b_in, layer, conv_w, mh_norm_w, w_mem_kv, w_proj_m, w_proj_c, w_proj_x, w_out,
           ln_w, ln_b, *, alpha):
    batch, seq, d = x.shape
    mem_len = mem.shape[1]
    tokens = batch * seq
    x2 = x.reshape(tokens, d)

    def bseg(off, width):
        return b_in[off:off + width].reshape(1, width)

    xb, if_pre = _xcast_if(x2, w_in_t, bseg(OFF_MI, V7X_LANES), layer, bm=512)

    bm = 1024

    def proj(offs, width, epilogue, out_dtypes, bn, name, **kw):
        return _proj(xb, w_in_t, [bseg(o, width) for o in offs], layer, offs, width, epilogue, out_dtypes,
                     transposed=True, bm=bm, bn=bn, name=name, **kw)

    ident = lambda u: (u,)
    (qkv,) = proj([OFF_Q], 2 * M_QK + M_V, ident, [BF16], 512, "proj_qkv")
    (gate_m,) = proj([OFF_MO, OFF_MZ], M_V, lambda mo, mz: (_sigmoid(mo) * _silu(mz),), [BF16], 256,
                     "proj_mgate")
    (p_conv,) = proj([OFF_CC, OFF_CX], CONV_WIDTH, lambda cc, cx: (cc * cx,), [BF16], 256, "proj_conv_p")
    (g_conv,) = proj([OFF_CB, OFF_CZ], CONV_WIDTH, lambda cb, cz: (cb * _silu(cz),), [BF16], 256,
                     "proj_conv_g")
    xq, sxz = proj([OFF_XQ, OFF_XZ], X_W, lambda q, z: (q, _silu(z)), [BF16, BF16], 256, "proj_xattn")
    gates, wm, wc, wx, wo = proj([OFF_G], N_BRANCH * D_MODEL, lambda u: (_sigmoid(u),), [BF16], 512,
                                 "proj_gates", casts=(w_proj_m, w_proj_c, w_proj_x, w_out))

    mem2 = mem.reshape(batch * mem_len, d).astype(BF16)
    (kv,) = _proj(mem2, w_mem_kv, None, layer, [0], 2 * X_W, ident, [BF16], transposed=False,
                  bm=batch * mem_len, bn=512, name="proj_memkv")
    y_x = _xattn(xq, sxz, kv, batch=batch, seq=seq, mem_len=mem_len, bs=512)

    n_if = 2 * M_HEADS
    gates_t = if_pre[:, :n_if].reshape(batch, seq, n_if).transpose(0, 2, 1)
    y_m = _mlstm(qkv, gates_t, gate_m, mh_norm_w.reshape(1, M_V), batch=batch, seq=seq)

    y_c = _conv(p_conv, g_conv, conv_w, batch=batch, seq=seq, bn=512)

    merged = _merge(y_m, y_c, y_x, wm, wc, wx, gates, bm=512, bn=512)
    out = _out(merged, wo, x2, ln_w.reshape(1, d), ln_b.reshape(1, d), bm=512, bn=512, alpha=alpha)
    return out.reshape(batch, seq, d)


def kernel(x, mem, w_in, b_in, conv_w, mh_norm_w, w_mem_kv, w_proj_m, w_proj_c, w_proj_x, w_out, ln_w, ln_b):
    depth = w_in.shape[0]
    alpha = (2 * depth) ** 0.25
    w_in_t = jnp.swapaxes(w_in, 1, 2)
    for l in range(depth):
        x = _layer(x, mem, w_in_t, b_in[l], l, conv_w[l], mh_norm_w[l], w_mem_kv, w_proj_m,
                   w_proj_c, w_proj_x, w_out, ln_w[l], ln_b[l], alpha=alpha)
    return x
```

```python
import functools

import jax
import jax.numpy as jnp
from jax import lax
from jax.experimental import pallas as pl
from jax.experimental.pallas import tpu as pltpu

F32 = jnp.float32
BF16 = jnp.bfloat16

D_MODEL = 4096
M_HEADS = 8
M_QK_DIM = 256
M_V_DIM = 512
M_QK = M_HEADS * M_QK_DIM
M_V = M_HEADS * M_V_DIM
CONV_WIDTH = 2048
CONV_K = 3
X_HEADS = 4
X_HEAD_DIM = 512
X_W = X_HEADS * X_HEAD_DIM
N_BRANCH = 3
LN_EPS = 1e-5

OFF_Q = 0
OFF_K = OFF_Q + M_QK
OFF_V = OFF_K + M_QK
OFF_MO = OFF_V + M_V
OFF_MZ = OFF_MO + M_V
OFF_MI = OFF_MZ + M_V
OFF_MF = OFF_MI + M_HEADS
OFF_CB = OFF_MF + M_HEADS
OFF_CC = OFF_CB + CONV_WIDTH
OFF_CX = OFF_CC + CONV_WIDTH
OFF_CZ = OFF_CX + CONV_WIDTH
OFF_XQ = OFF_CZ + CONV_WIDTH
OFF_XZ = OFF_XQ + X_W
OFF_G = OFF_XZ + X_W
D_IN = OFF_G + N_BRANCH * D_MODEL

V7X_LANES = 128
V7X_VMEM_LIMIT = 56 * 1024 * 1024
MLSTM_CHUNK = 256
CAST_ROWS = 64
MLSTM_HEAD_GROUP = 8


def _sigmoid(x):
    return 0.5 * jnp.tanh(0.5 * x) + 0.5


def _silu(x):
    return x * _sigmoid(x)


def _log_sigmoid(x):
    return jnp.minimum(x, 0.0) - jnp.log1p(jnp.exp(-jnp.abs(x)))


def _proj_body(*refs, n_w, has_bias, transposed, epilogue, cast_runs):
    n_cast = len(cast_runs)
    x_ref = refs[0]
    w_refs = refs[1:1 + n_w]
    n_b = n_w if has_bias else 0
    b_refs = refs[1 + n_w:1 + n_w + n_b]
    n_in = 1 + n_w + n_b
    cast_in = refs[n_in:n_in + n_cast]
    o_refs = refs[n_in + n_cast:len(refs) - n_cast]
    cast_out = refs[len(refs) - n_cast:]
    step = pl.program_id(0) * pl.num_programs(1) + pl.program_id(1)
    for src, dst, (first, count) in zip(cast_in, cast_out, cast_runs):
        @pl.when((step >= first) & (step < first + count))
        def _(src=src, dst=dst):
            dst[...] = src[...].astype(dst.dtype)

    x = x_ref[...].astype(BF16)
    contract = (((1,), (1,)), ((), ())) if transposed else (((1,), (0,)), ((), ()))
    us = []
    for idx, w in enumerate(w_refs):
        u = lax.dot_general(x, w[...].astype(BF16), contract, preferred_element_type=F32)
        if has_bias:
            u = u + b_refs[idx][...]
        us.append(u)
    for o, val in zip(o_refs, epilogue(*us)):
        o[...] = val.astype(o.dtype)


def _proj(x, w, biases, layer, col_offs, width, epilogue, out_dtypes, *, transposed, bm, bn, name,
          casts=(), single_buffer_x=False):
    m, k = x.shape
    n_w = len(col_offs)
    n_j = width // bn
    x_mode = {"pipeline_mode": pl.Buffered(1)} if single_buffer_x else {}
    in_specs = [pl.BlockSpec((bm, k), lambda i, j: (i, 0), **x_mode)]
    if transposed:
        base = layer * w.shape[1]
        w = w.reshape(-1, k)
        in_specs += [pl.BlockSpec((pl.Element(bn), pl.Element(k)), lambda i, j, o=base + off: (pl.multiple_of(o + j * bn, 8), 0))
                     for off in col_offs]
    else:
        in_specs += [pl.BlockSpec((None, k, bn), lambda i, j, o=off // bn: (layer, 0, o + j))
                     for off in col_offs]
    operands = [x] + [w] * n_w
    if biases is not None:
        in_specs += [pl.BlockSpec((1, bn), lambda i, j: (0, j)) for _ in biases]
        operands += list(biases)
    out_specs = [pl.BlockSpec((bm, bn), lambda i, j: (i, j)) for _ in out_dtypes]
    out_shape = [jax.ShapeDtypeStruct((m, width), dt) for dt in out_dtypes]
    first_step = 0
    cast_runs = []
    for arr in casts:
        _, rows, cols = arr.shape
        n_slabs = rows // CAST_ROWS
        cast_runs.append((first_step, n_slabs))

        def slab(i, j, s=first_step, n=n_slabs):
            return jnp.clip(i * n_j + j - s, 0, n - 1)

        in_specs.append(pl.BlockSpec((None, CAST_ROWS, cols), lambda i, j, f=slab: (layer, f(i, j), 0)))
        out_specs.append(pl.BlockSpec((CAST_ROWS, cols), lambda i, j, f=slab: (f(i, j), 0)))
        out_shape.append(jax.ShapeDtypeStruct((rows, cols), BF16))
        operands.append(arr)
        first_step += n_slabs
    assert first_step <= (m // bm) * n_j, "not enough grid steps to walk the cast arrays"
    return pl.pallas_call(
        functools.partial(_proj_body, n_w=n_w, has_bias=biases is not None, transposed=transposed,
                          epilogue=epilogue, cast_runs=tuple(cast_runs)),
        grid=(m // bm, n_j), in_specs=in_specs, out_specs=out_specs, out_shape=out_shape,
        compiler_params=pltpu.CompilerParams(
            dimension_semantics=("parallel", "arbitrary"), vmem_limit_bytes=V7X_VMEM_LIMIT),
        name=name,
    )(*operands)


def _xcast_if_body(x_ref, w_ref, b_ref, xb_ref, if_ref):
    xb = x_ref[...].astype(BF16)
    xb_ref[...] = xb
    if_ref[...] = lax.dot_general(xb, w_ref[...].astype(BF16), (((1,), (1,)), ((), ())),
                                  preferred_element_type=F32) + b_ref[...]


def _xcast_if(x2, w_t, bias, layer, *, bm):
    m, k = x2.shape
    row0 = layer * w_t.shape[1] + OFF_MI
    return pl.pallas_call(
        _xcast_if_body,
        grid=(m // bm,),
        in_specs=[pl.BlockSpec((bm, k), lambda i: (i, 0)),
                  pl.BlockSpec((pl.Element(V7X_LANES), pl.Element(k)), lambda i: (row0, 0)),
                  pl.BlockSpec((1, V7X_LANES), lambda i: (0, 0))],
        out_specs=[pl.BlockSpec((bm, k), lambda i: (i, 0)),
                   pl.BlockSpec((bm, V7X_LANES), lambda i: (i, 0))],
        out_shape=[jax.ShapeDtypeStruct((m, k), BF16), jax.ShapeDtypeStruct((m, V7X_LANES), F32)],
        compiler_params=pltpu.CompilerParams(
            dimension_semantics=("parallel",), vmem_limit_bytes=V7X_VMEM_LIMIT),
        name="xcast_if",
    )(x2, w_t.reshape(-1, k), bias)


def _xattn_body(q_ref, k_ref, v_ref, z_ref, o_ref):
    scale = X_HEAD_DIM ** -0.5
    for h in range(X_HEADS):
        sl = slice(h * X_HEAD_DIM, (h + 1) * X_HEAD_DIM)
        s = lax.dot_general(q_ref[:, sl], k_ref[:, sl], (((1,), (1,)), ((), ())),
                            preferred_element_type=F32) * scale
        e = jnp.exp(s - jnp.max(s, axis=-1, keepdims=True))
        p = e / jnp.sum(e, axis=-1, keepdims=True)
        a = jnp.dot(p.astype(BF16), v_ref[:, sl], preferred_element_type=F32)
        o_ref[:, sl] = (a * z_ref[:, sl].astype(F32)).astype(o_ref.dtype)


def _xattn(xq, sxz, kv, *, batch, seq, mem_len, bs):
    nb = seq // bs
    return pl.pallas_call(
        _xattn_body,
        grid=(batch, nb),
        in_specs=[pl.BlockSpec((bs, X_W), lambda b, i: (b * nb + i, 0)),
                  pl.BlockSpec((mem_len, X_W), lambda b, i: (b, 0)),
                  pl.BlockSpec((mem_len, X_W), lambda b, i: (b, 1)),
                  pl.BlockSpec((bs, X_W), lambda b, i: (b * nb + i, 0))],
        out_specs=pl.BlockSpec((bs, X_W), lambda b, i: (b * nb + i, 0)),
        out_shape=jax.ShapeDtypeStruct((batch * seq, X_W), BF16),
        compiler_params=pltpu.CompilerParams(
            dimension_semantics=("parallel", "arbitrary"), vmem_limit_bytes=V7X_VMEM_LIMIT),
        name="xattn",
    )(xq, kv, kv, sxz)


def _mlstm_body(q_ref, k_ref, v_ref, g_ref, gate_ref, nw_ref, o_ref, c_ref, n_ref, m_ref):
    @pl.when(pl.program_id(2) == 0)
    def _():
        c_ref[...] = jnp.zeros_like(c_ref)
        n_ref[...] = jnp.zeros_like(n_ref)
        m_ref[...] = jnp.zeros_like(m_ref)

    for hg in range(MLSTM_HEAD_GROUP):
        qk_cols = slice(hg * M_QK_DIM, (hg + 1) * M_QK_DIM)
        v_cols = slice(hg * M_V_DIM, (hg + 1) * M_V_DIM)
        _mlstm_head(pl.program_id(1) * MLSTM_HEAD_GROUP + hg,
                    q_ref.at[:, qk_cols], k_ref.at[:, qk_cols], v_ref.at[:, v_cols], g_ref,
                    gate_ref.at[:, v_cols], nw_ref.at[:, v_cols], o_ref.at[:, v_cols],
                    c_ref.at[hg], n_ref.at[hg], m_ref.at[hg])


def _mlstm_head(h, q_ref, k_ref, v_ref, g_ref, gate_ref, nw_ref, o_ref, c_ref, n_ref, m_ref):
    L = MLSTM_CHUNK
    scale = M_QK_DIM ** -0.5
    q = q_ref[...]
    k = k_ref[...]
    v = v_ref[...]
    ig = g_ref[pl.ds(h, 1), :]
    lf = _log_sigmoid(g_ref[pl.ds(M_HEADS + h, 1), :])

    t_idx = lax.broadcasted_iota(jnp.int32, (L, L), 0)
    j_idx = lax.broadcasted_iota(jnp.int32, (L, L), 1)
    causal = j_idx <= t_idx
    eye = j_idx == t_idx
    b_col = jnp.sum(jnp.where(causal, lf, 0.0), axis=1, keepdims=True)
    b_row = jnp.sum(jnp.where(eye, b_col, 0.0), axis=0, keepdims=True)
    i_col = jnp.sum(jnp.where(eye, ig, 0.0), axis=1, keepdims=True)
    g = jnp.sum(lf, axis=1, keepdims=True)
    m_prev = m_ref[:, 0:1]

    log_d = jnp.where(causal, b_col - b_row + ig, -jnp.inf)
    m_inter = b_col + m_prev
    m_t = jnp.maximum(jnp.max(log_d, axis=1, keepdims=True), m_inter)
    s = lax.dot_general(q, k, (((1,), (1,)), ((), ())), preferred_element_type=F32)
    s = s * scale * jnp.exp(log_d - m_t)
    inter = jnp.exp(m_inter - m_t) * scale
    c_st = c_ref[...]
    num = (jnp.dot(s.astype(BF16), v, preferred_element_type=F32)
           + inter * jnp.dot(q, c_st.astype(BF16), preferred_element_type=F32))
    qn = jnp.sum(q.astype(F32) * n_ref[...], axis=1, keepdims=True)
    den = jnp.sum(s, axis=1, keepdims=True) + inter * qn
    hh = num / jnp.maximum(jnp.abs(den), jnp.exp(-m_t))

    w_col = g - b_col + i_col
    m_new = jnp.maximum(g + m_prev, jnp.max(w_col, axis=0, keepdims=True))
    decay = jnp.exp(g + m_prev - m_new)
    wk = jnp.exp(w_col - m_new) * k.astype(F32)
    c_ref[...] = decay * c_st + lax.dot_general(
        wk.astype(BF16), v, (((0,), (0,)), ((), ())), preferred_element_type=F32)
    n_ref[...] = decay * n_ref[...] + jnp.sum(wk, axis=0, keepdims=True)
    m_ref[...] = jnp.broadcast_to(m_new, m_ref.shape)

    mu = jnp.mean(hh, axis=1, keepdims=True)
    hc = hh - mu
    var = jnp.mean(hc * hc, axis=1, keepdims=True)
    y = hc * lax.rsqrt(var + LN_EPS) * nw_ref[...] * gate_ref[...].astype(F32)
    o_ref[...] = y.astype(o_ref.dtype)


def _mlstm(qkv, gates_t, gate_m, norm_w, *, batch, seq):
    L = MLSTM_CHUNK
    nc = seq // L
    G = MLSTM_HEAD_GROUP
    qk_w = G * M_QK_DIM
    v_w = G * M_V_DIM
    kq = M_QK // qk_w
    kv = (2 * M_QK) // v_w
    return pl.pallas_call(
        _mlstm_body,
        grid=(batch, M_HEADS // G, nc),
        in_specs=[pl.BlockSpec((L, qk_w), lambda b, h, c: (b * nc + c, h)),
                  pl.BlockSpec((L, qk_w), lambda b, h, c: (b * nc + c, kq + h)),
                  pl.BlockSpec((L, v_w), lambda b, h, c: (b * nc + c, kv + h)),
                  pl.BlockSpec((None, 2 * M_HEADS, L), lambda b, h, c: (b, 0, c)),
                  pl.BlockSpec((L, v_w), lambda b, h, c: (b * nc + c, h)),
                  pl.BlockSpec((1, v_w), lambda b, h, c: (0, h))],
        out_specs=pl.BlockSpec((L, v_w), lambda b, h, c: (b * nc + c, h)),
        out_shape=jax.ShapeDtypeStruct((batch * seq, M_V), BF16),
        scratch_shapes=[pltpu.VMEM((G, M_QK_DIM, M_V_DIM), F32),
                        pltpu.VMEM((G, 1, M_QK_DIM), F32),
                        pltpu.VMEM((G, 1, V7X_LANES), F32)],
        compiler_params=pltpu.CompilerParams(
            dimension_semantics=("parallel", "parallel", "arbitrary"),
            vmem_limit_bytes=V7X_VMEM_LIMIT),
        name="mlstm",
    )(qkv, qkv, qkv, gates_t, gate_m, norm_w)


def _conv_body(p_ref, gc_ref, w_ref, o_ref):
    p = p_ref[...].astype(F32)
    row = lax.broadcasted_iota(jnp.int32, p.shape, 0)
    acc = p * w_ref[CONV_K - 1:CONV_K, :]
    for d in range(1, CONV_K):
        shifted = jnp.where(row >= d, pltpu.roll(p, d, 0), 0.0)
        acc = acc + shifted * w_ref[CONV_K - 1 - d:CONV_K - d, :]
    o_ref[...] = (acc * gc_ref[...].astype(F32)).astype(o_ref.dtype)


def _conv(p, gc, conv_w, *, batch, seq, bn):
    return pl.pallas_call(
        _conv_body,
        grid=(batch, CONV_WIDTH // bn),
        in_specs=[pl.BlockSpec((seq, bn), lambda b, j: (b, j)),
                  pl.BlockSpec((seq, bn), lambda b, j: (b, j)),
                  pl.BlockSpec((CONV_K, bn), lambda b, j: (0, j))],
        out_specs=pl.BlockSpec((seq, bn), lambda b, j: (b, j)),
        out_shape=jax.ShapeDtypeStruct((batch * seq, CONV_WIDTH), BF16),
        compiler_params=pltpu.CompilerParams(
            dimension_semantics=("parallel", "arbitrary"), vmem_limit_bytes=V7X_VMEM_LIMIT),
        name="conv",
    )(p, gc, conv_w)


def _merge_body(ym_ref, yc_ref, yx_ref, wm_ref, wc_ref, wx_ref, gm_ref, gc_ref, gx_ref, o_ref):
    acc = gm_ref[...].astype(F32) * jnp.dot(ym_ref[...], wm_ref[...], preferred_element_type=F32)
    acc = acc + gc_ref[...].astype(F32) * jnp.dot(yc_ref[...], wc_ref[...], preferred_element_type=F32)
    acc = acc + gx_ref[...].astype(F32) * jnp.dot(yx_ref[...], wx_ref[...], preferred_element_type=F32)
    o_ref[...] = acc.astype(o_ref.dtype)


def _merge(y_m, y_c, y_x, wm, wc, wx, gates, *, bm, bn):
    m = y_m.shape[0]
    nb = D_MODEL // bn
    return pl.pallas_call(
        _merge_body,
        grid=(m // bm, nb),
        in_specs=[pl.BlockSpec((bm, M_V), lambda i, j: (i, 0)),
                  pl.BlockSpec((bm, CONV_WIDTH), lambda i, j: (i, 0)),
                  pl.BlockSpec((bm, X_W), lambda i, j: (i, 0)),
                  pl.BlockSpec((M_V, bn), lambda i, j: (0, j)),
                  pl.BlockSpec((CONV_WIDTH, bn), lambda i, j: (0, j)),
                  pl.BlockSpec((X_W, bn), lambda i, j: (0, j)),
                  pl.BlockSpec((bm, bn), lambda i, j: (i, j)),
                  pl.BlockSpec((bm, bn), lambda i, j: (i, nb + j)),
                  pl.BlockSpec((bm, bn), lambda i, j: (i, 2 * nb + j))],
        out_specs=pl.BlockSpec((bm, bn), lambda i, j: (i, j)),
        out_shape=jax.ShapeDtypeStruct((m, D_MODEL), BF16),
        compiler_params=pltpu.CompilerParams(
            dimension_semantics=("parallel", "arbitrary"), vmem_limit_bytes=V7X_VMEM_LIMIT),
        name="merge",
    )(y_m, y_c, y_x, wm, wc, wx, gates, gates, gates)


def _out_body(mg_ref, w_ref, x_ref, lw_ref, lb_ref, o_ref, *, bn, alpha):
    j = pl.program_id(1)
    z = alpha * x_ref[...] + jnp.dot(mg_ref[...], w_ref[...], preferred_element_type=F32)
    o_ref[:, pl.ds(pl.multiple_of(j * bn, bn), bn)] = z

    @pl.when(j == pl.num_programs(1) - 1)
    def _():
        zz = o_ref[...]
        mu = jnp.mean(zz, axis=1, keepdims=True)
        zc = zz - mu
        var = jnp.mean(zc * zc, axis=1, keepdims=True)
        o_ref[...] = zc * lax.rsqrt(var + LN_EPS) * lw_ref[...] + lb_ref[...]


def _out(merged, w_out, x2, ln_w, ln_b, *, bm, bn, alpha):
    m = merged.shape[0]
    return pl.pallas_call(
        functools.partial(_out_body, bn=bn, alpha=alpha),
        grid=(m // bm, D_MODEL // bn),
        in_specs=[pl.BlockSpec((bm, D_MODEL), lambda i, j: (i, 0)),
                  pl.BlockSpec((D_MODEL, bn), lambda i, j: (0, j)),
                  pl.BlockSpec((bm, bn), lambda i, j: (i, j)),
                  pl.BlockSpec((1, D_MODEL), lambda i, j: (0, 0)),
                  pl.BlockSpec((1, D_MODEL), lambda i, j: (0, 0))],
        out_specs=pl.BlockSpec((bm, D_MODEL), lambda i, j: (i, 0)),
        out_shape=jax.ShapeDtypeStruct((m, D_MODEL), F32),
        compiler_params=pltpu.CompilerParams(
            dimension_semantics=("parallel", "arbitrary"), vmem_limit_bytes=V7X_VMEM_LIMIT),
        name="outproj_ln",
    )(merged, w_out, x2, ln_w, ln_b)


def _layer(x, mem, w_in_t, b_in, layer, conv_w, mh_norm_w, w_mem_kv, w_proj_m, w_proj_c, w_proj_x, w_out,
           ln_w, ln_b, *, alpha):
    batch, seq, d = x.shape
    mem_len = mem.shape[1]
    tokens = batch * seq
    x2 = x.reshape(tokens, d)

    def bseg(off, width):
        return b_in[off:off + width].reshape(1, width)

    xb, if_pre = _xcast_if(x2, w_in_t, bseg(OFF_MI, V7X_LANES), layer, bm=512)

    def proj(offs, width, epilogue, out_dtypes, bn, name, bm=1024, **kw):
        return _proj(xb, w_in_t, [bseg(o, width) for o in offs], layer, offs, width, epilogue, out_dtypes,
                     transposed=True, bm=bm, bn=bn, name=name, **kw)

    ident = lambda u: (u,)
    (qkv,) = proj([OFF_Q], 2 * M_QK + M_V, ident, [BF16], 512, "proj_qkv", bm=2048, single_buffer_x=True)
    (gate_m,) = proj([OFF_MO, OFF_MZ], M_V, lambda mo, mz: (_sigmoid(mo) * _silu(mz),), [BF16], 256,
                     "proj_mgate")
    (p_conv,) = proj([OFF_CC, OFF_CX], CONV_WIDTH, lambda cc, cx: (cc * cx,), [BF16], 256, "proj_conv_p")
    (g_conv,) = proj([OFF_CB, OFF_CZ], CONV_WIDTH, lambda cb, cz: (cb * _silu(cz),), [BF16], 256,
                     "proj_conv_g")
    xq, sxz = proj([OFF_XQ, OFF_XZ], X_W, lambda q, z: (q, _silu(z)), [BF16, BF16], 256, "proj_xattn")
    gates, wm, wc, wx, wo = proj([OFF_G], N_BRANCH * D_MODEL, lambda u: (_sigmoid(u),), [BF16], 512,
                                 "proj_gates", casts=(w_proj_m, w_proj_c, w_proj_x, w_out))

    mem2 = mem.reshape(batch * mem_len, d)
    (kv,) = _proj(mem2, w_mem_kv, None, layer, [0], 2 * X_W, ident, [BF16], transposed=False,
                  bm=batch * mem_len, bn=512, name="proj_memkv", single_buffer_x=True)
    y_x = _xattn(xq, sxz, kv, batch=batch, seq=seq, mem_len=mem_len, bs=512)

    n_if = 2 * M_HEADS
    gates_t = if_pre[:, :n_if].reshape(batch, seq, n_if).transpose(0, 2, 1)
    y_m = _mlstm(qkv, gates_t, gate_m, mh_norm_w.reshape(1, M_V), batch=batch, seq=seq)

    y_c = _conv(p_conv, g_conv, conv_w, batch=batch, seq=seq, bn=512)

    merged = _merge(y_m, y_c, y_x, wm, wc, wx, gates, bm=1024, bn=256)
    out = _out(merged, wo, x2, ln_w.reshape(1, d), ln_b.reshape(1, d), bm=512, bn=512, alpha=alpha)
    return out.reshape(batch, seq, d)


def kernel(x, mem, w_in, b_in, conv_w, mh_norm_w, w_mem_kv, w_proj_m, w_proj_c, w_proj_x, w_out, ln_w, ln_b):
    depth = w_in.shape[0]
    alpha = (2 * depth) ** 0.25
    w_in_t = jnp.swapaxes(w_in, 1, 2)
    for l in range(depth):
        x = _layer(x, mem, w_in_t, b_in[l], l, conv_w[l], mh_norm_w[l], w_mem_kv, w_proj_m,
                   w_proj_c, w_proj_x, w_out, ln_w[l], ln_b[l], alpha=alpha)
    return x
```

```python
import functools

import jax
import jax.numpy as jnp
from jax import lax
from jax.experimental import pallas as pl
from jax.experimental.pallas import tpu as pltpu

F32 = jnp.float32
BF16 = jnp.bfloat16

D_MODEL = 4096
M_HEADS = 8
M_QK_DIM = 256
M_V_DIM = 512
M_QK = M_HEADS * M_QK_DIM
M_V = M_HEADS * M_V_DIM
CONV_WIDTH = 2048
CONV_K = 3
X_HEADS = 4
X_HEAD_DIM = 512
X_W = X_HEADS * X_HEAD_DIM
N_BRANCH = 3
LN_EPS = 1e-5

OFF_Q = 0
OFF_K = OFF_Q + M_QK
OFF_V = OFF_K + M_QK
OFF_MO = OFF_V + M_V
OFF_MZ = OFF_MO + M_V
OFF_MI = OFF_MZ + M_V
OFF_MF = OFF_MI + M_HEADS
OFF_CB = OFF_MF + M_HEADS
OFF_CC = OFF_CB + CONV_WIDTH
OFF_CX = OFF_CC + CONV_WIDTH
OFF_CZ = OFF_CX + CONV_WIDTH
OFF_XQ = OFF_CZ + CONV_WIDTH
OFF_XZ = OFF_XQ + X_W
OFF_G = OFF_XZ + X_W
D_IN = OFF_G + N_BRANCH * D_MODEL

V7X_LANES = 128
V7X_VMEM_LIMIT = 56 * 1024 * 1024
MLSTM_CHUNK = 256
LN_ROWS = 128
CAST_ROWS = 64
MLSTM_HEAD_GROUP = 8


def _sigmoid(x):
    return 0.5 * jnp.tanh(0.5 * x) + 0.5


def _silu(x):
    return x * _sigmoid(x)


def _log_sigmoid(x):
    return jnp.minimum(x, 0.0) - jnp.log1p(jnp.exp(-jnp.abs(x)))


def _proj_body(*refs, n_w, has_bias, transposed, epilogue, cast_runs):
    n_cast = len(cast_runs)
    x_ref = refs[0]
    w_refs = refs[1:1 + n_w]
    n_b = n_w if has_bias else 0
    b_refs = refs[1 + n_w:1 + n_w + n_b]
    n_in = 1 + n_w + n_b
    cast_in = refs[n_in:n_in + n_cast]
    o_refs = refs[n_in + n_cast:len(refs) - n_cast]
    cast_out = refs[len(refs) - n_cast:]
    step = pl.program_id(0) * pl.num_programs(1) + pl.program_id(1)
    for src, dst, (first, count) in zip(cast_in, cast_out, cast_runs):
        @pl.when((step >= first) & (step < first + count))
        def _(src=src, dst=dst):
            dst[...] = src[...].astype(dst.dtype)

    x = x_ref[...].astype(BF16)
    contract = (((1,), (1,)), ((), ())) if transposed else (((1,), (0,)), ((), ()))
    us = []
    for idx, w in enumerate(w_refs):
        u = lax.dot_general(x, w[...].astype(BF16), contract, preferred_element_type=F32)
        if has_bias:
            u = u + b_refs[idx][...]
        us.append(u)
    for o, val in zip(o_refs, epilogue(*us)):
        o[...] = val.astype(o.dtype)


def _proj(x, w, biases, layer, col_offs, width, epilogue, out_dtypes, *, transposed, bm, bn, name,
          casts=(), single_buffer_x=False):
    m, k = x.shape
    n_w = len(col_offs)
    n_j = width // bn
    x_mode = {"pipeline_mode": pl.Buffered(1)} if single_buffer_x else {}
    in_specs = [pl.BlockSpec((bm, k), lambda i, j: (i, 0), **x_mode)]
    if transposed:
        base = layer * w.shape[1]
        w = w.reshape(-1, k)
        in_specs += [pl.BlockSpec((pl.Element(bn), pl.Element(k)), lambda i, j, o=base + off: (pl.multiple_of(o + j * bn, 8), 0))
                     for off in col_offs]
    else:
        in_specs += [pl.BlockSpec((None, k, bn), lambda i, j, o=off // bn: (layer, 0, o + j))
                     for off in col_offs]
    operands = [x] + [w] * n_w
    if biases is not None:
        in_specs += [pl.BlockSpec((1, bn), lambda i, j: (0, j)) for _ in biases]
        operands += list(biases)
    out_specs = [pl.BlockSpec((bm, bn), lambda i, j: (i, j)) for _ in out_dtypes]
    out_shape = [jax.ShapeDtypeStruct((m, width), dt) for dt in out_dtypes]
    first_step = 0
    cast_runs = []
    for arr in casts:
        _, rows, cols = arr.shape
        n_slabs = rows // CAST_ROWS
        cast_runs.append((first_step, n_slabs))

        def slab(i, j, s=first_step, n=n_slabs):
            return jnp.clip(i * n_j + j - s, 0, n - 1)

        in_specs.append(pl.BlockSpec((None, CAST_ROWS, cols), lambda i, j, f=slab: (layer, f(i, j), 0)))
        out_specs.append(pl.BlockSpec((CAST_ROWS, cols), lambda i, j, f=slab: (f(i, j), 0)))
        out_shape.append(jax.ShapeDtypeStruct((rows, cols), BF16))
        operands.append(arr)
        first_step += n_slabs
    assert first_step <= (m // bm) * n_j, "not enough grid steps to walk the cast arrays"
    return pl.pallas_call(
        functools.partial(_proj_body, n_w=n_w, has_bias=biases is not None, transposed=transposed,
                          epilogue=epilogue, cast_runs=tuple(cast_runs)),
        grid=(m // bm, n_j), in_specs=in_specs, out_specs=out_specs, out_shape=out_shape,
        compiler_params=pltpu.CompilerParams(
            dimension_semantics=("parallel", "arbitrary"), vmem_limit_bytes=V7X_VMEM_LIMIT),
        name=name,
    )(*operands)


def _xcast_if_body(x_ref, w_ref, b_ref, xb_ref, if_ref):
    xb = x_ref[...].astype(BF16)
    xb_ref[...] = xb
    if_ref[...] = lax.dot_general(xb, w_ref[...].astype(BF16), (((1,), (1,)), ((), ())),
                                  preferred_element_type=F32) + b_ref[...]


def _xcast_if(x2, w_t, bias, layer, *, bm):
    m, k = x2.shape
    row0 = layer * w_t.shape[1] + OFF_MI
    return pl.pallas_call(
        _xcast_if_body,
        grid=(m // bm,),
        in_specs=[pl.BlockSpec((bm, k), lambda i: (i, 0)),
                  pl.BlockSpec((pl.Element(V7X_LANES), pl.Element(k)), lambda i: (row0, 0)),
                  pl.BlockSpec((1, V7X_LANES), lambda i: (0, 0))],
        out_specs=[pl.BlockSpec((bm, k), lambda i: (i, 0)),
                   pl.BlockSpec((bm, V7X_LANES), lambda i: (i, 0))],
        out_shape=[jax.ShapeDtypeStruct((m, k), BF16), jax.ShapeDtypeStruct((m, V7X_LANES), F32)],
        compiler_params=pltpu.CompilerParams(
            dimension_semantics=("parallel",), vmem_limit_bytes=V7X_VMEM_LIMIT),
        name="xcast_if",
    )(x2, w_t.reshape(-1, k), bias)


def _xattn_body(q_ref, k_ref, v_ref, z_ref, o_ref):
    scale = X_HEAD_DIM ** -0.5
    for h in range(X_HEADS):
        sl = slice(h * X_HEAD_DIM, (h + 1) * X_HEAD_DIM)
        s = lax.dot_general(q_ref[:, sl], k_ref[:, sl], (((1,), (1,)), ((), ())),
                            preferred_element_type=F32) * scale
        e = jnp.exp(s - jnp.max(s, axis=-1, keepdims=True))
        p = e / jnp.sum(e, axis=-1, keepdims=True)
        a = jnp.dot(p.astype(BF16), v_ref[:, sl], preferred_element_type=F32)
        o_ref[:, sl] = (a * z_ref[:, sl].astype(F32)).astype(o_ref.dtype)


def _xattn(xq, sxz, kv, *, batch, seq, mem_len, bs):
    nb = seq // bs
    return pl.pallas_call(
        _xattn_body,
        grid=(batch, nb),
        in_specs=[pl.BlockSpec((bs, X_W), lambda b, i: (b * nb + i, 0)),
                  pl.BlockSpec((mem_len, X_W), lambda b, i: (b, 0)),
                  pl.BlockSpec((mem_len, X_W), lambda b, i: (b, 1)),
                  pl.BlockSpec((bs, X_W), lambda b, i: (b * nb + i, 0))],
        out_specs=pl.BlockSpec((bs, X_W), lambda b, i: (b * nb + i, 0)),
        out_shape=jax.ShapeDtypeStruct((batch * seq, X_W), BF16),
        compiler_params=pltpu.CompilerParams(
            dimension_semantics=("parallel", "arbitrary"), vmem_limit_bytes=V7X_VMEM_LIMIT),
        name="xattn",
    )(xq, kv, kv, sxz)


def _mlstm_body(q_ref, k_ref, v_ref, g_ref, gate_ref, nw_ref, o_ref, c_ref, n_ref, m_ref):
    @pl.when(pl.program_id(2) == 0)
    def _():
        c_ref[...] = jnp.zeros_like(c_ref)
        n_ref[...] = jnp.zeros_like(n_ref)
        m_ref[...] = jnp.zeros_like(m_ref)

    for hg in range(MLSTM_HEAD_GROUP):
        qk_cols = slice(hg * M_QK_DIM, (hg + 1) * M_QK_DIM)
        v_cols = slice(hg * M_V_DIM, (hg + 1) * M_V_DIM)
        _mlstm_head(pl.program_id(1) * MLSTM_HEAD_GROUP + hg,
                    q_ref.at[:, qk_cols], k_ref.at[:, qk_cols], v_ref.at[:, v_cols], g_ref,
                    gate_ref.at[:, v_cols], nw_ref.at[:, v_cols], o_ref.at[:, v_cols],
                    c_ref.at[hg], n_ref.at[hg], m_ref.at[hg])


def _mlstm_head(h, q_ref, k_ref, v_ref, g_ref, gate_ref, nw_ref, o_ref, c_ref, n_ref, m_ref):
    L = MLSTM_CHUNK
    scale = M_QK_DIM ** -0.5
    q = q_ref[...]
    k = k_ref[...]
    v = v_ref[...]
    ig = g_ref[pl.ds(h, 1), :]
    lf = _log_sigmoid(g_ref[pl.ds(M_HEADS + h, 1), :])

    t_idx = lax.broadcasted_iota(jnp.int32, (L, L), 0)
    j_idx = lax.broadcasted_iota(jnp.int32, (L, L), 1)
    causal = j_idx <= t_idx
    eye = j_idx == t_idx
    b_col = jnp.sum(jnp.where(causal, lf, 0.0), axis=1, keepdims=True)
    b_row = jnp.sum(jnp.where(eye, b_col, 0.0), axis=0, keepdims=True)
    i_col = jnp.sum(jnp.where(eye, ig, 0.0), axis=1, keepdims=True)
    g = jnp.sum(lf, axis=1, keepdims=True)
    m_prev = m_ref[:, 0:1]

    log_d = jnp.where(causal, b_col - b_row + ig, -jnp.inf)
    m_inter = b_col + m_prev
    m_t = jnp.maximum(jnp.max(log_d, axis=1, keepdims=True), m_inter)
    s = lax.dot_general(q, k, (((1,), (1,)), ((), ())), preferred_element_type=F32)
    s = s * scale * jnp.exp(log_d - m_t)
    inter = jnp.exp(m_inter - m_t) * scale
    c_st = c_ref[...]
    num = (jnp.dot(s.astype(BF16), v, preferred_element_type=F32)
           + inter * jnp.dot(q, c_st.astype(BF16), preferred_element_type=F32))
    qn = jnp.sum(q.astype(F32) * n_ref[...], axis=1, keepdims=True)
    den = jnp.sum(s, axis=1, keepdims=True) + inter * qn
    hh = num / jnp.maximum(jnp.abs(den), jnp.exp(-m_t))

    w_col = g - b_col + i_col
    m_new = jnp.maximum(g + m_prev, jnp.max(w_col, axis=0, keepdims=True))
    decay = jnp.exp(g + m_prev - m_new)
    wk = jnp.exp(w_col - m_new) * k.astype(F32)
    c_ref[...] = decay * c_st + lax.dot_general(
        wk.astype(BF16), v, (((0,), (0,)), ((), ())), preferred_element_type=F32)
    n_ref[...] = decay * n_ref[...] + jnp.sum(wk, axis=0, keepdims=True)
    m_ref[...] = jnp.broadcast_to(m_new, m_ref.shape)

    mu = jnp.mean(hh, axis=1, keepdims=True)
    hc = hh - mu
    var = jnp.mean(hc * hc, axis=1, keepdims=True)
    y = hc * lax.rsqrt(var + LN_EPS) * nw_ref[...] * gate_ref[...].astype(F32)
    o_ref[...] = y.astype(o_ref.dtype)


def _mlstm(qkv, gates_t, gate_m, norm_w, *, batch, seq):
    L = MLSTM_CHUNK
    nc = seq // L
    G = MLSTM_HEAD_GROUP
    qk_w = G * M_QK_DIM
    v_w = G * M_V_DIM
    kq = M_QK // qk_w
    kv = (2 * M_QK) // v_w
    return pl.pallas_call(
        _mlstm_body,
        grid=(batch, M_HEADS // G, nc),
        in_specs=[pl.BlockSpec((L, qk_w), lambda b, h, c: (b * nc + c, h)),
                  pl.BlockSpec((L, qk_w), lambda b, h, c: (b * nc + c, kq + h)),
                  pl.BlockSpec((L, v_w), lambda b, h, c: (b * nc + c, kv + h)),
                  pl.BlockSpec((None, 2 * M_HEADS, L), lambda b, h, c: (b, 0, c)),
                  pl.BlockSpec((L, v_w), lambda b, h, c: (b * nc + c, h)),
                  pl.BlockSpec((1, v_w), lambda b, h, c: (0, h))],
        out_specs=pl.BlockSpec((L, v_w), lambda b, h, c: (b * nc + c, h)),
        out_shape=jax.ShapeDtypeStruct((batch * seq, M_V), BF16),
        scratch_shapes=[pltpu.VMEM((G, M_QK_DIM, M_V_DIM), F32),
                        pltpu.VMEM((G, 1, M_QK_DIM), F32),
                        pltpu.VMEM((G, 1, V7X_LANES), F32)],
        compiler_params=pltpu.CompilerParams(
            dimension_semantics=("parallel", "parallel", "arbitrary"),
            vmem_limit_bytes=V7X_VMEM_LIMIT),
        name="mlstm",
    )(qkv, qkv, qkv, gates_t, gate_m, norm_w)


def _conv_body(p_ref, gc_ref, w_ref, o_ref):
    p = p_ref[...].astype(F32)
    row = lax.broadcasted_iota(jnp.int32, p.shape, 0)
    acc = p * w_ref[CONV_K - 1:CONV_K, :]
    for d in range(1, CONV_K):
        shifted = jnp.where(row >= d, pltpu.roll(p, d, 0), 0.0)
        acc = acc + shifted * w_ref[CONV_K - 1 - d:CONV_K - d, :]
    o_ref[...] = (acc * gc_ref[...].astype(F32)).astype(o_ref.dtype)


def _conv(p, gc, conv_w, *, batch, seq, bn):
    return pl.pallas_call(
        _conv_body,
        grid=(batch, CONV_WIDTH // bn),
        in_specs=[pl.BlockSpec((seq, bn), lambda b, j: (b, j)),
                  pl.BlockSpec((seq, bn), lambda b, j: (b, j)),
                  pl.BlockSpec((CONV_K, bn), lambda b, j: (0, j))],
        out_specs=pl.BlockSpec((seq, bn), lambda b, j: (b, j)),
        out_shape=jax.ShapeDtypeStruct((batch * seq, CONV_WIDTH), BF16),
        compiler_params=pltpu.CompilerParams(
            dimension_semantics=("parallel", "arbitrary"), vmem_limit_bytes=V7X_VMEM_LIMIT),
        name="conv",
    )(p, gc, conv_w)


def _merge_body(ym_ref, yc_ref, yx_ref, wm_ref, wc_ref, wx_ref, gm_ref, gc_ref, gx_ref, o_ref):
    acc = gm_ref[...].astype(F32) * jnp.dot(ym_ref[...], wm_ref[...], preferred_element_type=F32)
    acc = acc + gc_ref[...].astype(F32) * jnp.dot(yc_ref[...], wc_ref[...], preferred_element_type=F32)
    acc = acc + gx_ref[...].astype(F32) * jnp.dot(yx_ref[...], wx_ref[...], preferred_element_type=F32)
    o_ref[...] = acc.astype(o_ref.dtype)


def _merge(y_m, y_c, y_x, wm, wc, wx, gates, *, bm, bn):
    m = y_m.shape[0]
    nb = D_MODEL // bn
    return pl.pallas_call(
        _merge_body,
        grid=(m // bm, nb),
        in_specs=[pl.BlockSpec((bm, M_V), lambda i, j: (i, 0)),
                  pl.BlockSpec((bm, CONV_WIDTH), lambda i, j: (i, 0)),
                  pl.BlockSpec((bm, X_W), lambda i, j: (i, 0)),
                  pl.BlockSpec((M_V, bn), lambda i, j: (0, j)),
                  pl.BlockSpec((CONV_WIDTH, bn), lambda i, j: (0, j)),
                  pl.BlockSpec((X_W, bn), lambda i, j: (0, j)),
                  pl.BlockSpec((bm, bn), lambda i, j: (i, j)),
                  pl.BlockSpec((bm, bn), lambda i, j: (i, nb + j)),
                  pl.BlockSpec((bm, bn), lambda i, j: (i, 2 * nb + j))],
        out_specs=pl.BlockSpec((bm, bn), lambda i, j: (i, j)),
        out_shape=jax.ShapeDtypeStruct((m, D_MODEL), BF16),
        compiler_params=pltpu.CompilerParams(
            dimension_semantics=("parallel", "arbitrary"), vmem_limit_bytes=V7X_VMEM_LIMIT),
        name="merge",
    )(y_m, y_c, y_x, wm, wc, wx, gates, gates, gates)


def _out_body(mg_ref, w_ref, x_ref, lw_ref, lb_ref, o_ref, *, bn, alpha):
    j = pl.program_id(1)
    z = alpha * x_ref[...] + jnp.dot(mg_ref[...], w_ref[...], preferred_element_type=F32)
    o_ref[:, pl.ds(pl.multiple_of(j * bn, bn), bn)] = z

    @pl.when(j == pl.num_programs(1) - 1)
    def _():
        for r in range(0, o_ref.shape[0], LN_ROWS):
            rows = slice(r, r + LN_ROWS)
            zz = o_ref[rows, :]
            mu = jnp.mean(zz, axis=1, keepdims=True)
            zc = zz - mu
            var = jnp.mean(zc * zc, axis=1, keepdims=True)
            o_ref[rows, :] = zc * lax.rsqrt(var + LN_EPS) * lw_ref[...] + lb_ref[...]


def _out(merged, w_out, x2, ln_w, ln_b, *, bm, bn, alpha):
    m = merged.shape[0]
    return pl.pallas_call(
        functools.partial(_out_body, bn=bn, alpha=alpha),
        grid=(m // bm, D_MODEL // bn),
        in_specs=[pl.BlockSpec((bm, D_MODEL), lambda i, j: (i, 0)),
                  pl.BlockSpec((D_MODEL, bn), lambda i, j: (0, j)),
                  pl.BlockSpec((bm, bn), lambda i, j: (i, j)),
                  pl.BlockSpec((1, D_MODEL), lambda i, j: (0, 0)),
                  pl.BlockSpec((1, D_MODEL), lambda i, j: (0, 0))],
        out_specs=pl.BlockSpec((bm, D_MODEL), lambda i, j: (i, 0)),
        out_shape=jax.ShapeDtypeStruct((m, D_MODEL), F32),
        compiler_params=pltpu.CompilerParams(
            dimension_semantics=("parallel", "arbitrary"), vmem_limit_bytes=V7X_VMEM_LIMIT),
        name="outproj_ln",
    )(merged, w_out, x2, ln_w, ln_b)


def _layer(x, mem, w_in_t, b_in, layer, conv_w, mh_norm_w, w_mem_kv, w_proj_m, w_proj_c, w_proj_x, w_out,
           ln_w, ln_b, *, alpha):
    batch, seq, d = x.shape
    mem_len = mem.shape[1]
    tokens = batch * seq
    x2 = x.reshape(tokens, d)

    def bseg(off, width):
        return b_in[off:off + width].reshape(1, width)

    xb, if_pre = _xcast_if(x2, w_in_t, bseg(OFF_MI, V7X_LANES), layer, bm=512)

    def proj(offs, width, epilogue, out_dtypes, bn, name, bm=1024, **kw):
        return _proj(xb, w_in_t, [bseg(o, width) for o in offs], layer, offs, width, epilogue, out_dtypes,
                     transposed=True, bm=bm, bn=bn, name=name, **kw)

    ident = lambda u: (u,)
    (qkv,) = proj([OFF_Q], 2 * M_QK + M_V, ident, [BF16], 512, "proj_qkv", bm=2048, single_buffer_x=True)
    (gate_m,) = proj([OFF_MO, OFF_MZ], M_V, lambda mo, mz: (_sigmoid(mo) * _silu(mz),), [BF16], 256,
                     "proj_mgate")
    (p_conv,) = proj([OFF_CC, OFF_CX], CONV_WIDTH, lambda cc, cx: (cc * cx,), [BF16], 256, "proj_conv_p")
    (g_conv,) = proj([OFF_CB, OFF_CZ], CONV_WIDTH, lambda cb, cz: (cb * _silu(cz),), [BF16], 256,
                     "proj_conv_g")
    xq, sxz = proj([OFF_XQ, OFF_XZ], X_W, lambda q, z: (q, _silu(z)), [BF16, BF16], 256, "proj_xattn")
    gates, wm, wc, wx, wo = proj([OFF_G], N_BRANCH * D_MODEL, lambda u: (_sigmoid(u),), [BF16], 512,
                                 "proj_gates", casts=(w_proj_m, w_proj_c, w_proj_x, w_out))

    mem2 = mem.reshape(batch * mem_len, d)
    (kv,) = _proj(mem2, w_mem_kv, None, layer, [0], 2 * X_W, ident, [BF16], transposed=False,
                  bm=batch * mem_len, bn=512, name="proj_memkv", single_buffer_x=True)
    y_x = _xattn(xq, sxz, kv, batch=batch, seq=seq, mem_len=mem_len, bs=1024)

    n_if = 2 * M_HEADS
    gates_t = if_pre[:, :n_if].reshape(batch, seq, n_if).transpose(0, 2, 1)
    y_m = _mlstm(qkv, gates_t, gate_m, mh_norm_w.reshape(1, M_V), batch=batch, seq=seq)

    y_c = _conv(p_conv, g_conv, conv_w, batch=batch, seq=seq, bn=512)

    merged = _merge(y_m, y_c, y_x, wm, wc, wx, gates, bm=1024, bn=256)
    out = _out(merged, wo, x2, ln_w.reshape(1, d), ln_b.reshape(1, d), bm=512, bn=1024, alpha=alpha)
    return out.reshape(batch, seq, d)


def kernel(x, mem, w_in, b_in, conv_w, mh_norm_w, w_mem_kv, w_proj_m, w_proj_c, w_proj_x, w_out, ln_w, ln_b):
    depth = w_in.shape[0]
    alpha = (2 * depth) ** 0.25
    w_in_t = jnp.swapaxes(w_in, 1, 2)
    for l in range(depth):
        x = _layer(x, mem, w_in_t, b_in[l], l, conv_w[l], mh_norm_w[l], w_mem_kv, w_proj_m,
                   w_proj_c, w_proj_x, w_out, ln_w[l], ln_b[l], alpha=alpha)
    return x
```

```python
import functools

import jax
import jax.numpy as jnp
from jax import lax
from jax.experimental import pallas as pl
from jax.experimental.pallas import tpu as pltpu

F32 = jnp.float32
BF16 = jnp.bfloat16

D_MODEL = 4096
M_HEADS = 8
M_QK_DIM = 256
M_V_DIM = 512
M_QK = M_HEADS * M_QK_DIM
M_V = M_HEADS * M_V_DIM
CONV_WIDTH = 2048
CONV_K = 3
X_HEADS = 4
X_HEAD_DIM = 512
X_W = X_HEADS * X_HEAD_DIM
N_BRANCH = 3
LN_EPS = 1e-5

OFF_Q = 0
OFF_K = OFF_Q + M_QK
OFF_V = OFF_K + M_QK
OFF_MO = OFF_V + M_V
OFF_MZ = OFF_MO + M_V
OFF_MI = OFF_MZ + M_V
OFF_MF = OFF_MI + M_HEADS
OFF_CB = OFF_MF + M_HEADS
OFF_CC = OFF_CB + CONV_WIDTH
OFF_CX = OFF_CC + CONV_WIDTH
OFF_CZ = OFF_CX + CONV_WIDTH
OFF_XQ = OFF_CZ + CONV_WIDTH
OFF_XZ = OFF_XQ + X_W
OFF_G = OFF_XZ + X_W
D_IN = OFF_G + N_BRANCH * D_MODEL

V7X_LANES = 128
V7X_VMEM_LIMIT = 56 * 1024 * 1024
MLSTM_CHUNK = 256
LN_ROWS = 128
CAST_ROWS = 64
MLSTM_HEAD_GROUP = 8


def _sigmoid(x):
    return 0.5 * jnp.tanh(0.5 * x) + 0.5


def _silu(x):
    return x * _sigmoid(x)


def _log_sigmoid(x):
    return jnp.minimum(x, 0.0) - jnp.log1p(jnp.exp(-jnp.abs(x)))


def _proj_body(*refs, n_w, has_bias, transposed, epilogue, cast_runs):
    n_cast = len(cast_runs)
    x_ref = refs[0]
    w_refs = refs[1:1 + n_w]
    n_b = n_w if has_bias else 0
    b_refs = refs[1 + n_w:1 + n_w + n_b]
    n_in = 1 + n_w + n_b
    cast_in = refs[n_in:n_in + n_cast]
    o_refs = refs[n_in + n_cast:len(refs) - n_cast]
    cast_out = refs[len(refs) - n_cast:]
    step = pl.program_id(0) * pl.num_programs(1) + pl.program_id(1)
    for src, dst, (first, count) in zip(cast_in, cast_out, cast_runs):
        @pl.when((step >= first) & (step < first + count))
        def _(src=src, dst=dst):
            dst[...] = src[...].astype(dst.dtype)

    x = x_ref[...].astype(BF16)
    contract = (((1,), (1,)), ((), ())) if transposed else (((1,), (0,)), ((), ()))
    us = []
    for idx, w in enumerate(w_refs):
        u = lax.dot_general(x, w[...].astype(BF16), contract, preferred_element_type=F32)
        if has_bias:
            u = u + b_refs[idx][...]
        us.append(u)
    for o, val in zip(o_refs, epilogue(*us)):
        o[...] = val.astype(o.dtype)


def _proj(x, w, biases, layer, col_offs, width, epilogue, out_dtypes, *, transposed, bm, bn, name,
          casts=(), single_buffer_x=False):
    m, k = x.shape
    n_w = len(col_offs)
    n_j = width // bn
    x_mode = {"pipeline_mode": pl.Buffered(1)} if single_buffer_x else {}
    in_specs = [pl.BlockSpec((bm, k), lambda i, j: (i, 0), **x_mode)]
    if transposed:
        base = layer * w.shape[1]
        w = w.reshape(-1, k)
        in_specs += [pl.BlockSpec((pl.Element(bn), pl.Element(k)), lambda i, j, o=base + off: (pl.multiple_of(o + j * bn, 8), 0))
                     for off in col_offs]
    else:
        in_specs += [pl.BlockSpec((None, k, bn), lambda i, j, o=off // bn: (layer, 0, o + j))
                     for off in col_offs]
    operands = [x] + [w] * n_w
    if biases is not None:
        in_specs += [pl.BlockSpec((1, bn), lambda i, j: (0, j)) for _ in biases]
        operands += list(biases)
    out_specs = [pl.BlockSpec((bm, bn), lambda i, j: (i, j)) for _ in out_dtypes]
    out_shape = [jax.ShapeDtypeStruct((m, width), dt) for dt in out_dtypes]
    first_step = 0
    cast_runs = []
    for arr in casts:
        _, rows, cols = arr.shape
        n_slabs = rows // CAST_ROWS
        cast_runs.append((first_step, n_slabs))

        def slab(i, j, s=first_step, n=n_slabs):
            return jnp.clip(i * n_j + j - s, 0, n - 1)

        in_specs.append(pl.BlockSpec((None, CAST_ROWS, cols), lambda i, j, f=slab: (layer, f(i, j), 0)))
        out_specs.append(pl.BlockSpec((CAST_ROWS, cols), lambda i, j, f=slab: (f(i, j), 0)))
        out_shape.append(jax.ShapeDtypeStruct((rows, cols), BF16))
        operands.append(arr)
        first_step += n_slabs
    assert first_step <= (m // bm) * n_j, "not enough grid steps to walk the cast arrays"
    return pl.pallas_call(
        functools.partial(_proj_body, n_w=n_w, has_bias=biases is not None, transposed=transposed,
                          epilogue=epilogue, cast_runs=tuple(cast_runs)),
        grid=(m // bm, n_j), in_specs=in_specs, out_specs=out_specs, out_shape=out_shape,
        compiler_params=pltpu.CompilerParams(
            dimension_semantics=("parallel", "arbitrary"), vmem_limit_bytes=V7X_VMEM_LIMIT),
        name=name,
    )(*operands)


def _xcast_if_body(x_ref, w_ref, b_ref, xb_ref, if_ref):
    xb = x_ref[...].astype(BF16)
    xb_ref[...] = xb
    if_ref[...] = lax.dot_general(xb, w_ref[...].astype(BF16), (((1,), (1,)), ((), ())),
                                  preferred_element_type=F32) + b_ref[...]


def _xcast_if(x2, w_t, bias, layer, *, bm):
    m, k = x2.shape
    row0 = layer * w_t.shape[1] + OFF_MI
    return pl.pallas_call(
        _xcast_if_body,
        grid=(m // bm,),
        in_specs=[pl.BlockSpec((bm, k), lambda i: (i, 0)),
                  pl.BlockSpec((pl.Element(V7X_LANES), pl.Element(k)), lambda i: (row0, 0)),
                  pl.BlockSpec((1, V7X_LANES), lambda i: (0, 0))],
        out_specs=[pl.BlockSpec((bm, k), lambda i: (i, 0)),
                   pl.BlockSpec((bm, V7X_LANES), lambda i: (i, 0))],
        out_shape=[jax.ShapeDtypeStruct((m, k), BF16), jax.ShapeDtypeStruct((m, V7X_LANES), F32)],
        compiler_params=pltpu.CompilerParams(
            dimension_semantics=("parallel",), vmem_limit_bytes=V7X_VMEM_LIMIT),
        name="xcast_if",
    )(x2, w_t.reshape(-1, k), bias)


def _xattn_body(q_ref, k_ref, v_ref, z_ref, o_ref):
    scale = X_HEAD_DIM ** -0.5
    for h in range(X_HEADS):
        sl = slice(h * X_HEAD_DIM, (h + 1) * X_HEAD_DIM)
        s = lax.dot_general(q_ref[:, sl], k_ref[:, sl], (((1,), (1,)), ((), ())),
                            preferred_element_type=F32) * scale
        e = jnp.exp(s - jnp.max(s, axis=-1, keepdims=True))
        p = e / jnp.sum(e, axis=-1, keepdims=True)
        a = jnp.dot(p.astype(BF16), v_ref[:, sl], preferred_element_type=F32)
        o_ref[:, sl] = (a * z_ref[:, sl].astype(F32)).astype(o_ref.dtype)


def _xattn(xq, sxz, kv, *, batch, seq, mem_len, bs):
    nb = seq // bs
    return pl.pallas_call(
        _xattn_body,
        grid=(batch, nb),
        in_specs=[pl.BlockSpec((bs, X_W), lambda b, i: (b * nb + i, 0)),
                  pl.BlockSpec((mem_len, X_W), lambda b, i: (b, 0)),
                  pl.BlockSpec((mem_len, X_W), lambda b, i: (b, 1)),
                  pl.BlockSpec((bs, X_W), lambda b, i: (b * nb + i, 0))],
        out_specs=pl.BlockSpec((bs, X_W), lambda b, i: (b * nb + i, 0)),
        out_shape=jax.ShapeDtypeStruct((batch * seq, X_W), BF16),
        compiler_params=pltpu.CompilerParams(
            dimension_semantics=("parallel", "arbitrary"), vmem_limit_bytes=V7X_VMEM_LIMIT),
        name="xattn",
    )(xq, kv, kv, sxz)


def _mlstm_body(q_ref, k_ref, v_ref, g_ref, gate_ref, nw_ref, o_ref, c_ref, n_ref, m_ref):
    @pl.when(pl.program_id(2) == 0)
    def _():
        c_ref[...] = jnp.zeros_like(c_ref)
        n_ref[...] = jnp.zeros_like(n_ref)
        m_ref[...] = jnp.zeros_like(m_ref)

    for hg in range(MLSTM_HEAD_GROUP):
        qk_cols = slice(hg * M_QK_DIM, (hg + 1) * M_QK_DIM)
        v_cols = slice(hg * M_V_DIM, (hg + 1) * M_V_DIM)
        _mlstm_head(pl.program_id(1) * MLSTM_HEAD_GROUP + hg,
                    q_ref.at[:, qk_cols], k_ref.at[:, qk_cols], v_ref.at[:, v_cols], g_ref,
                    gate_ref.at[:, v_cols], nw_ref.at[:, v_cols], o_ref.at[:, v_cols],
                    c_ref.at[hg], n_ref.at[hg], m_ref.at[hg])


def _mlstm_head(h, q_ref, k_ref, v_ref, g_ref, gate_ref, nw_ref, o_ref, c_ref, n_ref, m_ref):
    L = MLSTM_CHUNK
    scale = M_QK_DIM ** -0.5
    q = q_ref[...]
    k = k_ref[...]
    v = v_ref[...]
    ig = g_ref[pl.ds(h, 1), :]
    lf = _log_sigmoid(g_ref[pl.ds(M_HEADS + h, 1), :])

    t_idx = lax.broadcasted_iota(jnp.int32, (L, L), 0)
    j_idx = lax.broadcasted_iota(jnp.int32, (L, L), 1)
    causal = j_idx <= t_idx
    eye = j_idx == t_idx
    b_col = jnp.sum(jnp.where(causal, lf, 0.0), axis=1, keepdims=True)
    b_row = jnp.sum(jnp.where(eye, b_col, 0.0), axis=0, keepdims=True)
    i_col = jnp.sum(jnp.where(eye, ig, 0.0), axis=1, keepdims=True)
    g = jnp.sum(lf, axis=1, keepdims=True)
    m_prev = m_ref[:, 0:1]

    log_d = jnp.where(causal, b_col - b_row + ig, -jnp.inf)
    m_inter = b_col + m_prev
    m_t = jnp.maximum(jnp.max(log_d, axis=1, keepdims=True), m_inter)
    s = lax.dot_general(q, k, (((1,), (1,)), ((), ())), preferred_element_type=F32)
    s = s * scale * jnp.exp(log_d - m_t)
    inter = jnp.exp(m_inter - m_t) * scale
    c_st = c_ref[...]
    num = (jnp.dot(s.astype(BF16), v, preferred_element_type=F32)
           + inter * jnp.dot(q, c_st.astype(BF16), preferred_element_type=F32))
    qn = jnp.sum(q.astype(F32) * n_ref[...], axis=1, keepdims=True)
    den = jnp.sum(s, axis=1, keepdims=True) + inter * qn
    hh = num / jnp.maximum(jnp.abs(den), jnp.exp(-m_t))

    w_col = g - b_col + i_col
    m_new = jnp.maximum(g + m_prev, jnp.max(w_col, axis=0, keepdims=True))
    decay = jnp.exp(g + m_prev - m_new)
    wk = jnp.exp(w_col - m_new) * k.astype(F32)
    c_ref[...] = decay * c_st + lax.dot_general(
        wk.astype(BF16), v, (((0,), (0,)), ((), ())), preferred_element_type=F32)
    n_ref[...] = decay * n_ref[...] + jnp.sum(wk, axis=0, keepdims=True)
    m_ref[...] = jnp.broadcast_to(m_new, m_ref.shape)

    mu = jnp.mean(hh, axis=1, keepdims=True)
    hc = hh - mu
    var = jnp.mean(hc * hc, axis=1, keepdims=True)
    y = hc * lax.rsqrt(var + LN_EPS) * nw_ref[...] * gate_ref[...].astype(F32)
    o_ref[...] = y.astype(o_ref.dtype)


def _mlstm(qkv, gates_t, gate_m, norm_w, *, batch, seq):
    L = MLSTM_CHUNK
    nc = seq // L
    G = MLSTM_HEAD_GROUP
    qk_w = G * M_QK_DIM
    v_w = G * M_V_DIM
    kq = M_QK // qk_w
    kv = (2 * M_QK) // v_w
    return pl.pallas_call(
        _mlstm_body,
        grid=(batch, M_HEADS // G, nc),
        in_specs=[pl.BlockSpec((L, qk_w), lambda b, h, c: (b * nc + c, h)),
                  pl.BlockSpec((L, qk_w), lambda b, h, c: (b * nc + c, kq + h)),
                  pl.BlockSpec((L, v_w), lambda b, h, c: (b * nc + c, kv + h)),
                  pl.BlockSpec((None, 2 * M_HEADS, L), lambda b, h, c: (b, 0, c)),
                  pl.BlockSpec((L, v_w), lambda b, h, c: (b * nc + c, h)),
                  pl.BlockSpec((1, v_w), lambda b, h, c: (0, h))],
        out_specs=pl.BlockSpec((L, v_w), lambda b, h, c: (b * nc + c, h)),
        out_shape=jax.ShapeDtypeStruct((batch * seq, M_V), BF16),
        scratch_shapes=[pltpu.VMEM((G, M_QK_DIM, M_V_DIM), F32),
                        pltpu.VMEM((G, 1, M_QK_DIM), F32),
                        pltpu.VMEM((G, 1, V7X_LANES), F32)],
        compiler_params=pltpu.CompilerParams(
            dimension_semantics=("parallel", "parallel", "arbitrary"),
            vmem_limit_bytes=V7X_VMEM_LIMIT),
        name="mlstm",
    )(qkv, qkv, qkv, gates_t, gate_m, norm_w)


def _conv_body(p_ref, gc_ref, w_ref, o_ref):
    p = p_ref[...].astype(F32)
    row = lax.broadcasted_iota(jnp.int32, p.shape, 0)
    acc = p * w_ref[CONV_K - 1:CONV_K, :]
    for d in range(1, CONV_K):
        shifted = jnp.where(row >= d, pltpu.roll(p, d, 0), 0.0)
        acc = acc + shifted * w_ref[CONV_K - 1 - d:CONV_K - d, :]
    o_ref[...] = (acc * gc_ref[...].astype(F32)).astype(o_ref.dtype)


def _conv(p, gc, conv_w, *, batch, seq, bn):
    return pl.pallas_call(
        _conv_body,
        grid=(batch, CONV_WIDTH // bn),
        in_specs=[pl.BlockSpec((seq, bn), lambda b, j: (b, j)),
                  pl.BlockSpec((seq, bn), lambda b, j: (b, j)),
                  pl.BlockSpec((CONV_K, bn), lambda b, j: (0, j))],
        out_specs=pl.BlockSpec((seq, bn), lambda b, j: (b, j)),
        out_shape=jax.ShapeDtypeStruct((batch * seq, CONV_WIDTH), BF16),
        compiler_params=pltpu.CompilerParams(
            dimension_semantics=("parallel", "arbitrary"), vmem_limit_bytes=V7X_VMEM_LIMIT),
        name="conv",
    )(p, gc, conv_w)


def _merge_body(ym_ref, yc_ref, yx_ref, wm_ref, wc_ref, wx_ref, gm_ref, gc_ref, gx_ref, o_ref):
    acc = gm_ref[...].astype(F32) * jnp.dot(ym_ref[...], wm_ref[...], preferred_element_type=F32)
    acc = acc + gc_ref[...].astype(F32) * jnp.dot(yc_ref[...], wc_ref[...], preferred_element_type=F32)
    acc = acc + gx_ref[...].astype(F32) * jnp.dot(yx_ref[...], wx_ref[...], preferred_element_type=F32)
    o_ref[...] = acc.astype(o_ref.dtype)


def _merge(y_m, y_c, y_x, wm, wc, wx, gates, *, bm, bn):
    m = y_m.shape[0]
    nb = D_MODEL // bn
    return pl.pallas_call(
        _merge_body,
        grid=(m // bm, nb),
        in_specs=[pl.BlockSpec((bm, M_V), lambda i, j: (i, 0)),
                  pl.BlockSpec((bm, CONV_WIDTH), lambda i, j: (i, 0)),
                  pl.BlockSpec((bm, X_W), lambda i, j: (i, 0)),
                  pl.BlockSpec((M_V, bn), lambda i, j: (0, j)),
                  pl.BlockSpec((CONV_WIDTH, bn), lambda i, j: (0, j)),
                  pl.BlockSpec((X_W, bn), lambda i, j: (0, j)),
                  pl.BlockSpec((bm, bn), lambda i, j: (i, j)),
                  pl.BlockSpec((bm, bn), lambda i, j: (i, nb + j)),
                  pl.BlockSpec((bm, bn), lambda i, j: (i, 2 * nb + j))],
        out_specs=pl.BlockSpec((bm, bn), lambda i, j: (i, j)),
        out_shape=jax.ShapeDtypeStruct((m, D_MODEL), BF16),
        compiler_params=pltpu.CompilerParams(
            dimension_semantics=("parallel", "arbitrary"), vmem_limit_bytes=V7X_VMEM_LIMIT),
        name="merge",
    )(y_m, y_c, y_x, wm, wc, wx, gates, gates, gates)


def _out_body(mg_ref, w_ref, x_ref, lw_ref, lb_ref, o_ref, *, bn, alpha):
    j = pl.program_id(1)
    z = alpha * x_ref[...] + jnp.dot(mg_ref[...], w_ref[...], preferred_element_type=F32)
    o_ref[:, pl.ds(pl.multiple_of(j * bn, bn), bn)] = z

    @pl.when(j == pl.num_programs(1) - 1)
    def _():
        for r in range(0, o_ref.shape[0], LN_ROWS):
            rows = slice(r, r + LN_ROWS)
            zz = o_ref[rows, :]
            mu = jnp.mean(zz, axis=1, keepdims=True)
            zc = zz - mu
            var = jnp.mean(zc * zc, axis=1, keepdims=True)
            o_ref[rows, :] = zc * lax.rsqrt(var + LN_EPS) * lw_ref[...] + lb_ref[...]


def _out(merged, w_out, x2, ln_w, ln_b, *, bm, bn, alpha):
    m = merged.shape[0]
    return pl.pallas_call(
        functools.partial(_out_body, bn=bn, alpha=alpha),
        grid=(m // bm, D_MODEL // bn),
        in_specs=[pl.BlockSpec((bm, D_MODEL), lambda i, j: (i, 0)),
                  pl.BlockSpec((D_MODEL, bn), lambda i, j: (0, j)),
                  pl.BlockSpec((bm, bn), lambda i, j: (i, j)),
                  pl.BlockSpec((1, D_MODEL), lambda i, j: (0, 0)),
                  pl.BlockSpec((1, D_MODEL), lambda i, j: (0, 0))],
        out_specs=pl.BlockSpec((bm, D_MODEL), lambda i, j: (i, 0)),
        out_shape=jax.ShapeDtypeStruct((m, D_MODEL), F32),
        compiler_params=pltpu.CompilerParams(
            dimension_semantics=("parallel", "arbitrary"), vmem_limit_bytes=V7X_VMEM_LIMIT),
        name="outproj_ln",
    )(merged, w_out, x2, ln_w, ln_b)


def _layer(x, mem, w_in_t, b_in, layer, conv_w, mh_norm_w, w_mem_kv, w_proj_m, w_proj_c, w_proj_x, w_out,
           ln_w, ln_b, *, alpha):
    batch, seq, d = x.shape
    mem_len = mem.shape[1]
    tokens = batch * seq
    x2 = x.reshape(tokens, d)

    def bseg(off, width):
        return b_in[off:off + width].reshape(1, width)

    xb, if_pre = _xcast_if(x2, w_in_t, bseg(OFF_MI, V7X_LANES), layer, bm=512)

    def proj(offs, width, epilogue, out_dtypes, bn, name, bm=1024, **kw):
        return _proj(xb, w_in_t, [bseg(o, width) for o in offs], layer, offs, width, epilogue, out_dtypes,
                     transposed=True, bm=bm, bn=bn, name=name, **kw)

    ident = lambda u: (u,)
    (qkv,) = proj([OFF_Q], 2 * M_QK + M_V, ident, [BF16], 512, "proj_qkv", bm=2048, single_buffer_x=True)
    (gate_m,) = proj([OFF_MO, OFF_MZ], M_V, lambda mo, mz: (_sigmoid(mo.astype(BF16)) * _silu(mz.astype(BF16)),), [BF16], 256,
                     "proj_mgate")
    (p_conv,) = proj([OFF_CC, OFF_CX], CONV_WIDTH, lambda cc, cx: (cc * cx,), [BF16], 256, "proj_conv_p")
    (g_conv,) = proj([OFF_CB, OFF_CZ], CONV_WIDTH, lambda cb, cz: (cb.astype(BF16) * _silu(cz.astype(BF16)),), [BF16], 256,
                     "proj_conv_g")
    xq, sxz = proj([OFF_XQ, OFF_XZ], X_W, lambda q, z: (q, _silu(z.astype(BF16))), [BF16, BF16], 256, "proj_xattn")
    gates, wm, wc, wx, wo = proj([OFF_G], N_BRANCH * D_MODEL, lambda u: (_sigmoid(u.astype(BF16)),), [BF16], 512,
                                 "proj_gates", casts=(w_proj_m, w_proj_c, w_proj_x, w_out))

    mem2 = mem.reshape(batch * mem_len, d)
    (kv,) = _proj(mem2, w_mem_kv, None, layer, [0], 2 * X_W, ident, [BF16], transposed=False,
                  bm=batch * mem_len, bn=512, name="proj_memkv", single_buffer_x=True)
    y_x = _xattn(xq, sxz, kv, batch=batch, seq=seq, mem_len=mem_len, bs=1024)

    n_if = 2 * M_HEADS
    gates_t = if_pre[:, :n_if].reshape(batch, seq, n_if).transpose(0, 2, 1)
    y_m = _mlstm(qkv, gates_t, gate_m, mh_norm_w.reshape(1, M_V), batch=batch, seq=seq)

    y_c = _conv(p_conv, g_conv, conv_w, batch=batch, seq=seq, bn=512)

    merged = _merge(y_m, y_c, y_x, wm, wc, wx, gates, bm=1024, bn=256)
    out = _out(merged, wo, x2, ln_w.reshape(1, d), ln_b.reshape(1, d), bm=512, bn=1024, alpha=alpha)
    return out.reshape(batch, seq, d)


def kernel(x, mem, w_in, b_in, conv_w, mh_norm_w, w_mem_kv, w_proj_m, w_proj_c, w_proj_x, w_out, ln_w, ln_b):
    depth = w_in.shape[0]
    alpha = (2 * depth) ** 0.25
    w_in_t = jnp.swapaxes(w_in, 1, 2)
    for l in range(depth):
        x = _layer(x, mem, w_in_t, b_in[l], l, conv_w[l], mh_norm_w[l], w_mem_kv, w_proj_m,
                   w_proj_c, w_proj_x, w_out, ln_w[l], ln_b[l], alpha=alpha)
    return x
```

```python
import functools

import jax
import jax.numpy as jnp
from jax import lax
from jax.experimental import pallas as pl
from jax.experimental.pallas import tpu as pltpu

F32 = jnp.float32
BF16 = jnp.bfloat16

D_MODEL = 4096
M_HEADS = 8
M_QK_DIM = 256
M_V_DIM = 512
M_QK = M_HEADS * M_QK_DIM
M_V = M_HEADS * M_V_DIM
CONV_WIDTH = 2048
CONV_K = 3
X_HEADS = 4
X_HEAD_DIM = 512
X_W = X_HEADS * X_HEAD_DIM
N_BRANCH = 3
LN_EPS = 1e-5

OFF_Q = 0
OFF_K = OFF_Q + M_QK
OFF_V = OFF_K + M_QK
OFF_MO = OFF_V + M_V
OFF_MZ = OFF_MO + M_V
OFF_MI = OFF_MZ + M_V
OFF_MF = OFF_MI + M_HEADS
OFF_CB = OFF_MF + M_HEADS
OFF_CC = OFF_CB + CONV_WIDTH
OFF_CX = OFF_CC + CONV_WIDTH
OFF_CZ = OFF_CX + CONV_WIDTH
OFF_XQ = OFF_CZ + CONV_WIDTH
OFF_XZ = OFF_XQ + X_W
OFF_G = OFF_XZ + X_W
D_IN = OFF_G + N_BRANCH * D_MODEL

V7X_LANES = 128
V7X_VMEM_LIMIT = 56 * 1024 * 1024
MLSTM_CHUNK = 256
LN_ROWS = 128
CAST_ROWS = 64
MLSTM_HEAD_GROUP = 8


def _sigmoid(x):
    return 0.5 * jnp.tanh(0.5 * x) + 0.5


def _silu(x):
    return x * _sigmoid(x)


def _log_sigmoid(x):
    return jnp.minimum(x, 0.0) - jnp.log1p(jnp.exp(-jnp.abs(x)))


def _proj_body(*refs, n_w, has_bias, transposed, epilogue, cast_runs):
    n_cast = len(cast_runs)
    x_ref = refs[0]
    w_refs = refs[1:1 + n_w]
    n_b = n_w if has_bias else 0
    b_refs = refs[1 + n_w:1 + n_w + n_b]
    n_in = 1 + n_w + n_b
    cast_in = refs[n_in:n_in + n_cast]
    o_refs = refs[n_in + n_cast:len(refs) - n_cast]
    cast_out = refs[len(refs) - n_cast:]
    step = pl.program_id(0) * pl.num_programs(1) + pl.program_id(1)
    for src, dst, (first, count) in zip(cast_in, cast_out, cast_runs):
        @pl.when((step >= first) & (step < first + count))
        def _(src=src, dst=dst):
            dst[...] = src[...].astype(dst.dtype)

    x = x_ref[...].astype(BF16)
    contract = (((1,), (1,)), ((), ())) if transposed else (((1,), (0,)), ((), ()))
    us = []
    for idx, w in enumerate(w_refs):
        u = lax.dot_general(x, w[...].astype(BF16), contract, preferred_element_type=F32)
        if has_bias:
            col0 = pl.multiple_of(pl.program_id(1) * u.shape[1], V7X_LANES)
            u = u + b_refs[idx][:, pl.ds(col0, u.shape[1])]
        us.append(u)
    for o, val in zip(o_refs, epilogue(*us)):
        o[...] = val.astype(o.dtype)


def _proj(x, w, biases, layer, col_offs, width, epilogue, out_dtypes, *, transposed, bm, bn, name,
          casts=(), single_buffer_x=False):
    m, k = x.shape
    n_w = len(col_offs)
    n_j = width // bn
    x_mode = {"pipeline_mode": pl.Buffered(1)} if single_buffer_x else {}
    in_specs = [pl.BlockSpec((bm, k), lambda i, j: (i, 0), **x_mode)]
    if transposed:
        base = layer * w.shape[1]
        w = w.reshape(-1, k)
        in_specs += [pl.BlockSpec((pl.Element(bn), pl.Element(k)), lambda i, j, o=base + off: (pl.multiple_of(o + j * bn, 8), 0))
                     for off in col_offs]
    else:
        in_specs += [pl.BlockSpec((None, k, bn), lambda i, j, o=off // bn: (layer, 0, o + j))
                     for off in col_offs]
    operands = [x] + [w] * n_w
    if biases is not None:
        in_specs += [pl.BlockSpec((1, width), lambda i, j: (0, 0)) for _ in biases]
        operands += list(biases)
    out_specs = [pl.BlockSpec((bm, bn), lambda i, j: (i, j)) for _ in out_dtypes]
    out_shape = [jax.ShapeDtypeStruct((m, width), dt) for dt in out_dtypes]
    first_step = 0
    cast_runs = []
    for arr in casts:
        _, rows, cols = arr.shape
        n_slabs = rows // CAST_ROWS
        cast_runs.append((first_step, n_slabs))

        def slab(i, j, s=first_step, n=n_slabs):
            return jnp.clip(i * n_j + j - s, 0, n - 1)

        in_specs.append(pl.BlockSpec((None, CAST_ROWS, cols), lambda i, j, f=slab: (layer, f(i, j), 0)))
        out_specs.append(pl.BlockSpec((CAST_ROWS, cols), lambda i, j, f=slab: (f(i, j), 0)))
        out_shape.append(jax.ShapeDtypeStruct((rows, cols), BF16))
        operands.append(arr)
        first_step += n_slabs
    assert first_step <= (m // bm) * n_j, "not enough grid steps to walk the cast arrays"
    return pl.pallas_call(
        functools.partial(_proj_body, n_w=n_w, has_bias=biases is not None, transposed=transposed,
                          epilogue=epilogue, cast_runs=tuple(cast_runs)),
        grid=(m // bm, n_j), in_specs=in_specs, out_specs=out_specs, out_shape=out_shape,
        compiler_params=pltpu.CompilerParams(
            dimension_semantics=("parallel", "arbitrary"), vmem_limit_bytes=V7X_VMEM_LIMIT),
        name=name,
    )(*operands)


def _xcast_if_body(x_ref, w_ref, b_ref, xb_ref, if_ref):
    xb = x_ref[...].astype(BF16)
    xb_ref[...] = xb
    if_ref[...] = lax.dot_general(xb, w_ref[...].astype(BF16), (((1,), (1,)), ((), ())),
                                  preferred_element_type=F32) + b_ref[...]


def _xcast_if(x2, w_t, bias, layer, *, bm):
    m, k = x2.shape
    row0 = layer * w_t.shape[1] + OFF_MI
    return pl.pallas_call(
        _xcast_if_body,
        grid=(m // bm,),
        in_specs=[pl.BlockSpec((bm, k), lambda i: (i, 0)),
                  pl.BlockSpec((pl.Element(V7X_LANES), pl.Element(k)), lambda i: (row0, 0)),
                  pl.BlockSpec((1, V7X_LANES), lambda i: (0, 0))],
        out_specs=[pl.BlockSpec((bm, k), lambda i: (i, 0)),
                   pl.BlockSpec((bm, V7X_LANES), lambda i: (i, 0))],
        out_shape=[jax.ShapeDtypeStruct((m, k), BF16), jax.ShapeDtypeStruct((m, V7X_LANES), F32)],
        compiler_params=pltpu.CompilerParams(
            dimension_semantics=("parallel",), vmem_limit_bytes=V7X_VMEM_LIMIT),
        name="xcast_if",
    )(x2, w_t.reshape(-1, k), bias)


def _xattn_body(q_ref, k_ref, v_ref, z_ref, o_ref):
    scale = X_HEAD_DIM ** -0.5
    for h in range(X_HEADS):
        sl = slice(h * X_HEAD_DIM, (h + 1) * X_HEAD_DIM)
        s = lax.dot_general(q_ref[:, sl], k_ref[:, sl], (((1,), (1,)), ((), ())),
                            preferred_element_type=F32) * scale
        e = jnp.exp(s - jnp.max(s, axis=-1, keepdims=True))
        p = e / jnp.sum(e, axis=-1, keepdims=True)
        a = jnp.dot(p.astype(BF16), v_ref[:, sl], preferred_element_type=F32)
        o_ref[:, sl] = (a * z_ref[:, sl].astype(F32)).astype(o_ref.dtype)


def _xattn(xq, sxz, kv, *, batch, seq, mem_len, bs):
    nb = seq // bs
    return pl.pallas_call(
        _xattn_body,
        grid=(batch, nb),
        in_specs=[pl.BlockSpec((bs, X_W), lambda b, i: (b * nb + i, 0)),
                  pl.BlockSpec((mem_len, X_W), lambda b, i: (b, 0)),
                  pl.BlockSpec((mem_len, X_W), lambda b, i: (b, 1)),
                  pl.BlockSpec((bs, X_W), lambda b, i: (b * nb + i, 0))],
        out_specs=pl.BlockSpec((bs, X_W), lambda b, i: (b * nb + i, 0)),
        out_shape=jax.ShapeDtypeStruct((batch * seq, X_W), BF16),
        compiler_params=pltpu.CompilerParams(
            dimension_semantics=("parallel", "arbitrary"), vmem_limit_bytes=V7X_VMEM_LIMIT),
        name="xattn",
    )(xq, kv, kv, sxz)


def _mlstm_body(q_ref, k_ref, v_ref, g_ref, gate_ref, nw_ref, o_ref, c_ref, n_ref, m_ref):
    @pl.when(pl.program_id(2) == 0)
    def _():
        c_ref[...] = jnp.zeros_like(c_ref)
        n_ref[...] = jnp.zeros_like(n_ref)
        m_ref[...] = jnp.zeros_like(m_ref)

    for hg in range(MLSTM_HEAD_GROUP):
        qk_cols = slice(hg * M_QK_DIM, (hg + 1) * M_QK_DIM)
        v_cols = slice(hg * M_V_DIM, (hg + 1) * M_V_DIM)
        _mlstm_head(pl.program_id(1) * MLSTM_HEAD_GROUP + hg,
                    q_ref.at[:, qk_cols], k_ref.at[:, qk_cols], v_ref.at[:, v_cols], g_ref,
                    gate_ref.at[:, v_cols], nw_ref.at[:, v_cols], o_ref.at[:, v_cols],
                    c_ref.at[hg], n_ref.at[hg], m_ref.at[hg])


def _mlstm_head(h, q_ref, k_ref, v_ref, g_ref, gate_ref, nw_ref, o_ref, c_ref, n_ref, m_ref):
    L = MLSTM_CHUNK
    scale = M_QK_DIM ** -0.5
    q = q_ref[...]
    k = k_ref[...]
    v = v_ref[...]
    ig = g_ref[pl.ds(h, 1), :]
    lf = _log_sigmoid(g_ref[pl.ds(M_HEADS + h, 1), :])

    t_idx = lax.broadcasted_iota(jnp.int32, (L, L), 0)
    j_idx = lax.broadcasted_iota(jnp.int32, (L, L), 1)
    causal = j_idx <= t_idx
    eye = j_idx == t_idx
    b_col = jnp.sum(jnp.where(causal, lf, 0.0), axis=1, keepdims=True)
    b_row = jnp.sum(jnp.where(eye, b_col, 0.0), axis=0, keepdims=True)
    i_col = jnp.sum(jnp.where(eye, ig, 0.0), axis=1, keepdims=True)
    g = jnp.sum(lf, axis=1, keepdims=True)
    m_prev = m_ref[:, 0:1]

    log_d = jnp.where(causal, b_col - b_row + ig, -jnp.inf)
    m_inter = b_col + m_prev
    m_t = jnp.maximum(jnp.max(log_d, axis=1, keepdims=True), m_inter)
    s = lax.dot_general(q, k, (((1,), (1,)), ((), ())), preferred_element_type=F32)
    s = s * scale * jnp.exp(log_d - m_t)
    inter = jnp.exp(m_inter - m_t) * scale
    c_st = c_ref[...]
    num = (jnp.dot(s.astype(BF16), v, preferred_element_type=F32)
           + inter * jnp.dot(q, c_st.astype(BF16), preferred_element_type=F32))
    qn = jnp.sum(q.astype(F32) * n_ref[...], axis=1, keepdims=True)
    den = jnp.sum(s, axis=1, keepdims=True) + inter * qn
    hh = num / jnp.maximum(jnp.abs(den), jnp.exp(-m_t))

    w_col = g - b_col + i_col
    m_new = jnp.maximum(g + m_prev, jnp.max(w_col, axis=0, keepdims=True))
    decay = jnp.exp(g + m_prev - m_new)
    wk = jnp.exp(w_col - m_new) * k.astype(F32)
    c_ref[...] = decay * c_st + lax.dot_general(
        wk.astype(BF16), v, (((0,), (0,)), ((), ())), preferred_element_type=F32)
    n_ref[...] = decay * n_ref[...] + jnp.sum(wk, axis=0, keepdims=True)
    m_ref[...] = jnp.broadcast_to(m_new, m_ref.shape)

    mu = jnp.mean(hh, axis=1, keepdims=True)
    hc = hh - mu
    var = jnp.mean(hc * hc, axis=1, keepdims=True)
    y = hc * lax.rsqrt(var + LN_EPS) * nw_ref[...] * gate_ref[...].astype(F32)
    o_ref[...] = y.astype(o_ref.dtype)


def _mlstm(qkv, gates_t, gate_m, norm_w, *, batch, seq):
    L = MLSTM_CHUNK
    nc = seq // L
    G = MLSTM_HEAD_GROUP
    qk_w = G * M_QK_DIM
    v_w = G * M_V_DIM
    kq = M_QK // qk_w
    kv = (2 * M_QK) // v_w
    return pl.pallas_call(
        _mlstm_body,
        grid=(batch, M_HEADS // G, nc),
        in_specs=[pl.BlockSpec((L, qk_w), lambda b, h, c: (b * nc + c, h)),
                  pl.BlockSpec((L, qk_w), lambda b, h, c: (b * nc + c, kq + h)),
                  pl.BlockSpec((L, v_w), lambda b, h, c: (b * nc + c, kv + h)),
                  pl.BlockSpec((None, 2 * M_HEADS, L), lambda b, h, c: (b, 0, c)),
                  pl.BlockSpec((L, v_w), lambda b, h, c: (b * nc + c, h)),
                  pl.BlockSpec((1, v_w), lambda b, h, c: (0, h))],
        out_specs=pl.BlockSpec((L, v_w), lambda b, h, c: (b * nc + c, h)),
        out_shape=jax.ShapeDtypeStruct((batch * seq, M_V), BF16),
        scratch_shapes=[pltpu.VMEM((G, M_QK_DIM, M_V_DIM), F32),
                        pltpu.VMEM((G, 1, M_QK_DIM), F32),
                        pltpu.VMEM((G, 1, V7X_LANES), F32)],
        compiler_params=pltpu.CompilerParams(
            dimension_semantics=("parallel", "parallel", "arbitrary"),
            vmem_limit_bytes=V7X_VMEM_LIMIT),
        name="mlstm",
    )(qkv, qkv, qkv, gates_t, gate_m, norm_w)


def _conv_body(p_ref, gc_ref, w_ref, o_ref):
    p = p_ref[...].astype(F32)
    row = lax.broadcasted_iota(jnp.int32, p.shape, 0)
    acc = p * w_ref[CONV_K - 1:CONV_K, :]
    for d in range(1, CONV_K):
        shifted = jnp.where(row >= d, pltpu.roll(p, d, 0), 0.0)
        acc = acc + shifted * w_ref[CONV_K - 1 - d:CONV_K - d, :]
    o_ref[...] = (acc * gc_ref[...].astype(F32)).astype(o_ref.dtype)


def _conv(p, gc, conv_w, *, batch, seq, bn):
    return pl.pallas_call(
        _conv_body,
        grid=(batch, CONV_WIDTH // bn),
        in_specs=[pl.BlockSpec((seq, bn), lambda b, j: (b, j)),
                  pl.BlockSpec((seq, bn), lambda b, j: (b, j)),
                  pl.BlockSpec((CONV_K, bn), lambda b, j: (0, j))],
        out_specs=pl.BlockSpec((seq, bn), lambda b, j: (b, j)),
        out_shape=jax.ShapeDtypeStruct((batch * seq, CONV_WIDTH), BF16),
        compiler_params=pltpu.CompilerParams(
            dimension_semantics=("parallel", "arbitrary"), vmem_limit_bytes=V7X_VMEM_LIMIT),
        name="conv",
    )(p, gc, conv_w)


def _merge_body(ym_ref, yc_ref, yx_ref, wm_ref, wc_ref, wx_ref, gm_ref, gc_ref, gx_ref, o_ref):
    acc = gm_ref[...].astype(F32) * jnp.dot(ym_ref[...], wm_ref[...], preferred_element_type=F32)
    acc = acc + gc_ref[...].astype(F32) * jnp.dot(yc_ref[...], wc_ref[...], preferred_element_type=F32)
    acc = acc + gx_ref[...].astype(F32) * jnp.dot(yx_ref[...], wx_ref[...], preferred_element_type=F32)
    o_ref[...] = acc.astype(o_ref.dtype)


def _merge(y_m, y_c, y_x, wm, wc, wx, gates, *, bm, bn):
    m = y_m.shape[0]
    nb = D_MODEL // bn
    return pl.pallas_call(
        _merge_body,
        grid=(m // bm, nb),
        in_specs=[pl.BlockSpec((bm, M_V), lambda i, j: (i, 0)),
                  pl.BlockSpec((bm, CONV_WIDTH), lambda i, j: (i, 0)),
                  pl.BlockSpec((bm, X_W), lambda i, j: (i, 0)),
                  pl.BlockSpec((M_V, bn), lambda i, j: (0, j)),
                  pl.BlockSpec((CONV_WIDTH, bn), lambda i, j: (0, j)),
                  pl.BlockSpec((X_W, bn), lambda i, j: (0, j)),
                  pl.BlockSpec((bm, bn), lambda i, j: (i, j)),
                  pl.BlockSpec((bm, bn), lambda i, j: (i, nb + j)),
                  pl.BlockSpec((bm, bn), lambda i, j: (i, 2 * nb + j))],
        out_specs=pl.BlockSpec((bm, bn), lambda i, j: (i, j)),
        out_shape=jax.ShapeDtypeStruct((m, D_MODEL), BF16),
        compiler_params=pltpu.CompilerParams(
            dimension_semantics=("parallel", "arbitrary"), vmem_limit_bytes=V7X_VMEM_LIMIT),
        name="merge",
    )(y_m, y_c, y_x, wm, wc, wx, gates, gates, gates)


def _out_body(mg_ref, w_ref, x_ref, lw_ref, lb_ref, o_ref, *, bn, alpha):
    j = pl.program_id(1)
    z = alpha * x_ref[...] + jnp.dot(mg_ref[...], w_ref[...], preferred_element_type=F32)
    o_ref[:, pl.ds(pl.multiple_of(j * bn, bn), bn)] = z

    @pl.when(j == pl.num_programs(1) - 1)
    def _():
        for r in range(0, o_ref.shape[0], LN_ROWS):
            rows = slice(r, r + LN_ROWS)
            zz = o_ref[rows, :]
            mu = jnp.mean(zz, axis=1, keepdims=True)
            zc = zz - mu
            var = jnp.mean(zc * zc, axis=1, keepdims=True)
            o_ref[rows, :] = zc * lax.rsqrt(var + LN_EPS) * lw_ref[...] + lb_ref[...]


def _out(merged, w_out, x2, ln_w, ln_b, *, bm, bn, alpha):
    m = merged.shape[0]
    return pl.pallas_call(
        functools.partial(_out_body, bn=bn, alpha=alpha),
        grid=(m // bm, D_MODEL // bn),
        in_specs=[pl.BlockSpec((bm, D_MODEL), lambda i, j: (i, 0)),
                  pl.BlockSpec((D_MODEL, bn), lambda i, j: (0, j)),
                  pl.BlockSpec((bm, bn), lambda i, j: (i, j)),
                  pl.BlockSpec((1, D_MODEL), lambda i, j: (0, 0)),
                  pl.BlockSpec((1, D_MODEL), lambda i, j: (0, 0))],
        out_specs=pl.BlockSpec((bm, D_MODEL), lambda i, j: (i, 0)),
        out_shape=jax.ShapeDtypeStruct((m, D_MODEL), F32),
        compiler_params=pltpu.CompilerParams(
            dimension_semantics=("parallel", "arbitrary"), vmem_limit_bytes=V7X_VMEM_LIMIT),
        name="outproj_ln",
    )(merged, w_out, x2, ln_w, ln_b)


def _layer(x, mem, w_in_t, b_in, layer, conv_w, mh_norm_w, w_mem_kv, w_proj_m, w_proj_c, w_proj_x, w_out,
           ln_w, ln_b, *, alpha):
    batch, seq, d = x.shape
    mem_len = mem.shape[1]
    tokens = batch * seq
    x2 = x.reshape(tokens, d)

    def bseg(off, width):
        return b_in[off:off + width].reshape(1, width)

    xb, if_pre = _xcast_if(x2, w_in_t, bseg(OFF_MI, V7X_LANES), layer, bm=512)

    def proj(offs, width, epilogue, out_dtypes, bn, name, bm=1024, **kw):
        return _proj(xb, w_in_t, [bseg(o, width) for o in offs], layer, offs, width, epilogue, out_dtypes,
                     transposed=True, bm=bm, bn=bn, name=name, **kw)

    ident = lambda u: (u,)
    (qkv,) = proj([OFF_Q], 2 * M_QK + M_V, ident, [BF16], 512, "proj_qkv", bm=2048, single_buffer_x=True)
    (gate_m,) = proj([OFF_MO, OFF_MZ], M_V, lambda mo, mz: (_sigmoid(mo.astype(BF16)) * _silu(mz.astype(BF16)),), [BF16], 256,
                     "proj_mgate")
    (p_conv,) = proj([OFF_CC, OFF_CX], CONV_WIDTH, lambda cc, cx: (cc * cx,), [BF16], 256, "proj_conv_p")
    (g_conv,) = proj([OFF_CB, OFF_CZ], CONV_WIDTH, lambda cb, cz: (cb.astype(BF16) * _silu(cz.astype(BF16)),), [BF16], 256,
                     "proj_conv_g")
    xq, sxz = proj([OFF_XQ, OFF_XZ], X_W, lambda q, z: (q, _silu(z.astype(BF16))), [BF16, BF16], 256, "proj_xattn")
    gates, wm, wc, wx, wo = proj([OFF_G], N_BRANCH * D_MODEL, lambda u: (_sigmoid(u.astype(BF16)),), [BF16], 512,
                                 "proj_gates", casts=(w_proj_m, w_proj_c, w_proj_x, w_out))

    mem2 = mem.reshape(batch * mem_len, d)
    (kv,) = _proj(mem2, w_mem_kv, None, layer, [0], 2 * X_W, ident, [BF16], transposed=False,
                  bm=batch * mem_len, bn=512, name="proj_memkv", single_buffer_x=True)
    y_x = _xattn(xq, sxz, kv, batch=batch, seq=seq, mem_len=mem_len, bs=1024)

    n_if = 2 * M_HEADS
    gates_t = if_pre[:, :n_if].reshape(batch, seq, n_if).transpose(0, 2, 1)
    y_m = _mlstm(qkv, gates_t, gate_m, mh_norm_w.reshape(1, M_V), batch=batch, seq=seq)

    y_c = _conv(p_conv, g_conv, conv_w, batch=batch, seq=seq, bn=512)

    merged = _merge(y_m, y_c, y_x, wm, wc, wx, gates, bm=1024, bn=256)
    out = _out(merged, wo, x2, ln_w.reshape(1, d), ln_b.reshape(1, d), bm=512, bn=1024, alpha=alpha)
    return out.reshape(batch, seq, d)


def kernel(x, mem, w_in, b_in, conv_w, mh_norm_w, w_mem_kv, w_proj_m, w_proj_c, w_proj_x, w_out, ln_w, ln_b):
    depth = w_in.shape[0]
    alpha = (2 * depth) ** 0.25
    w_in_t = jnp.swapaxes(w_in, 1, 2)
    for l in range(depth):
        x = _layer(x, mem, w_in_t, b_in[l], l, conv_w[l], mh_norm_w[l], w_mem_kv, w_proj_m,
                   w_proj_c, w_proj_x, w_out, ln_w[l], ln_b[l], alpha=alpha)
    return x
```

```python
import functools

import jax
import jax.numpy as jnp
from jax import lax
from jax.experimental import pallas as pl
from jax.experimental.pallas import tpu as pltpu

F32 = jnp.float32
BF16 = jnp.bfloat16

D_MODEL = 4096
M_HEADS = 8
M_QK_DIM = 256
M_V_DIM = 512
M_QK = M_HEADS * M_QK_DIM
M_V = M_HEADS * M_V_DIM
CONV_WIDTH = 2048
CONV_K = 3
X_HEADS = 4
X_HEAD_DIM = 512
X_W = X_HEADS * X_HEAD_DIM
N_BRANCH = 3
LN_EPS = 1e-5

OFF_Q = 0
OFF_K = OFF_Q + M_QK
OFF_V = OFF_K + M_QK
OFF_MO = OFF_V + M_V
OFF_MZ = OFF_MO + M_V
OFF_MI = OFF_MZ + M_V
OFF_MF = OFF_MI + M_HEADS
OFF_CB = OFF_MF + M_HEADS
OFF_CC = OFF_CB + CONV_WIDTH
OFF_CX = OFF_CC + CONV_WIDTH
OFF_CZ = OFF_CX + CONV_WIDTH
OFF_XQ = OFF_CZ + CONV_WIDTH
OFF_XZ = OFF_XQ + X_W
OFF_G = OFF_XZ + X_W
D_IN = OFF_G + N_BRANCH * D_MODEL

V7X_LANES = 128
V7X_VMEM_LIMIT = 56 * 1024 * 1024
MLSTM_CHUNK = 256
RING_SLOTS = 3
LN_ROWS = 128
CAST_ROWS = 64
MLSTM_HEAD_GROUP = 8


def _sigmoid(x):
    return 0.5 * jnp.tanh(0.5 * x) + 0.5


def _silu(x):
    return x * _sigmoid(x)


def _log_sigmoid(x):
    return jnp.minimum(x, 0.0) - jnp.log1p(jnp.exp(-jnp.abs(x)))


def _proj_body(*refs, n_w, has_bias, transposed, epilogue, cast_runs):
    n_cast = len(cast_runs)
    x_ref = refs[0]
    w_refs = refs[1:1 + n_w]
    n_b = n_w if has_bias else 0
    b_refs = refs[1 + n_w:1 + n_w + n_b]
    n_in = 1 + n_w + n_b
    cast_in = refs[n_in:n_in + n_cast]
    o_refs = refs[n_in + n_cast:len(refs) - n_cast]
    cast_out = refs[len(refs) - n_cast:]
    step = pl.program_id(0) * pl.num_programs(1) + pl.program_id(1)
    for src, dst, (first, count) in zip(cast_in, cast_out, cast_runs):
        @pl.when((step >= first) & (step < first + count))
        def _(src=src, dst=dst):
            dst[...] = src[...].astype(dst.dtype)

    x = x_ref[...].astype(BF16)
    contract = (((1,), (1,)), ((), ())) if transposed else (((1,), (0,)), ((), ()))
    us = []
    for idx, w in enumerate(w_refs):
        u = lax.dot_general(x, w[...].astype(BF16), contract, preferred_element_type=F32)
        if has_bias:
            col0 = pl.multiple_of(pl.program_id(1) * u.shape[1], V7X_LANES)
            u = u + b_refs[idx][:, pl.ds(col0, u.shape[1])]
        us.append(u)
    for o, val in zip(o_refs, epilogue(*us)):
        o[...] = val.astype(o.dtype)


def _proj(x, w, biases, layer, col_offs, width, epilogue, out_dtypes, *, transposed, bm, bn, name,
          casts=(), single_buffer_x=False):
    m, k = x.shape
    n_w = len(col_offs)
    n_j = width // bn
    x_mode = {"pipeline_mode": pl.Buffered(1)} if single_buffer_x else {}
    in_specs = [pl.BlockSpec((bm, k), lambda i, j: (i, 0), **x_mode)]
    if transposed:
        base = layer * w.shape[1]
        w = w.reshape(-1, k)
        in_specs += [pl.BlockSpec((pl.Element(bn), pl.Element(k)), lambda i, j, o=base + off: (pl.multiple_of(o + j * bn, 8), 0))
                     for off in col_offs]
    else:
        in_specs += [pl.BlockSpec((None, k, bn), lambda i, j, o=off // bn: (layer, 0, o + j))
                     for off in col_offs]
    operands = [x] + [w] * n_w
    if biases is not None:
        in_specs += [pl.BlockSpec((1, width), lambda i, j: (0, 0)) for _ in biases]
        operands += list(biases)
    out_specs = [pl.BlockSpec((bm, bn), lambda i, j: (i, j)) for _ in out_dtypes]
    out_shape = [jax.ShapeDtypeStruct((m, width), dt) for dt in out_dtypes]
    first_step = 0
    cast_runs = []
    for arr in casts:
        _, rows, cols = arr.shape
        n_slabs = rows // CAST_ROWS
        cast_runs.append((first_step, n_slabs))

        def slab(i, j, s=first_step, n=n_slabs):
            return jnp.clip(i * n_j + j - s, 0, n - 1)

        in_specs.append(pl.BlockSpec((None, CAST_ROWS, cols), lambda i, j, f=slab: (layer, f(i, j), 0)))
        out_specs.append(pl.BlockSpec((CAST_ROWS, cols), lambda i, j, f=slab: (f(i, j), 0)))
        out_shape.append(jax.ShapeDtypeStruct((rows, cols), BF16))
        operands.append(arr)
        first_step += n_slabs
    assert first_step <= (m // bm) * n_j, "not enough grid steps to walk the cast arrays"
    return pl.pallas_call(
        functools.partial(_proj_body, n_w=n_w, has_bias=biases is not None, transposed=transposed,
                          epilogue=epilogue, cast_runs=tuple(cast_runs)),
        grid=(m // bm, n_j), in_specs=in_specs, out_specs=out_specs, out_shape=out_shape,
        compiler_params=pltpu.CompilerParams(
            dimension_semantics=("parallel", "arbitrary"), vmem_limit_bytes=V7X_VMEM_LIMIT),
        name=name,
    )(*operands)


def _proj_ring_body(x_ref, w_hbm, b_ref, o_ref, w_buf, sem, *, row0, bn, n_j, n_steps):
    step = pl.program_id(0) * n_j + pl.program_id(1)

    def weight_copy(s):
        rows = pl.ds(pl.multiple_of(row0 + (s % n_j) * bn, 8), bn)
        slot = s % RING_SLOTS
        return pltpu.make_async_copy(w_hbm.at[rows, :], w_buf.at[slot], sem.at[slot])

    @pl.when(step == 0)
    def _():
        for ahead in range(RING_SLOTS - 1):
            weight_copy(step + ahead).start()

    @pl.when(step + RING_SLOTS - 1 < n_steps)
    def _():
        weight_copy(step + RING_SLOTS - 1).start()

    weight_copy(step).wait()
    w = w_buf[step % RING_SLOTS].astype(BF16)
    u = lax.dot_general(x_ref[...], w, (((1,), (1,)), ((), ())), preferred_element_type=F32)
    col0 = pl.multiple_of(pl.program_id(1) * bn, V7X_LANES)
    o_ref[...] = (u + b_ref[:, pl.ds(col0, bn)]).astype(o_ref.dtype)


def _proj_ring(x, w_rows, bias, row0, width, out_dtype, *, bm, bn, name):
    m, k = x.shape
    n_i, n_j = m // bm, width // bn
    assert n_i * n_j >= RING_SLOTS
    return pl.pallas_call(
        functools.partial(_proj_ring_body, row0=row0, bn=bn, n_j=n_j, n_steps=n_i * n_j),
        grid=(n_i, n_j),
        in_specs=[pl.BlockSpec((bm, k), lambda i, j: (i, 0)),
                  pl.BlockSpec(memory_space=pl.ANY),
                  pl.BlockSpec((1, width), lambda i, j: (0, 0))],
        out_specs=pl.BlockSpec((bm, bn), lambda i, j: (i, j)),
        out_shape=jax.ShapeDtypeStruct((m, width), out_dtype),
        scratch_shapes=[pltpu.VMEM((RING_SLOTS, bn, k), F32), pltpu.SemaphoreType.DMA((RING_SLOTS,))],
        compiler_params=pltpu.CompilerParams(
            dimension_semantics=("arbitrary", "arbitrary"), vmem_limit_bytes=V7X_VMEM_LIMIT),
        name=name,
    )(x, w_rows, bias)


def _xcast_if_body(x_ref, w_ref, b_ref, xb_ref, if_ref):
    xb = x_ref[...].astype(BF16)
    xb_ref[...] = xb
    if_ref[...] = lax.dot_general(xb, w_ref[...].astype(BF16), (((1,), (1,)), ((), ())),
                                  preferred_element_type=F32) + b_ref[...]


def _xcast_if(x2, w_t, bias, layer, *, bm):
    m, k = x2.shape
    row0 = layer * w_t.shape[1] + OFF_MI
    return pl.pallas_call(
        _xcast_if_body,
        grid=(m // bm,),
        in_specs=[pl.BlockSpec((bm, k), lambda i: (i, 0)),
                  pl.BlockSpec((pl.Element(V7X_LANES), pl.Element(k)), lambda i: (row0, 0)),
                  pl.BlockSpec((1, V7X_LANES), lambda i: (0, 0))],
        out_specs=[pl.BlockSpec((bm, k), lambda i: (i, 0)),
                   pl.BlockSpec((bm, V7X_LANES), lambda i: (i, 0))],
        out_shape=[jax.ShapeDtypeStruct((m, k), BF16), jax.ShapeDtypeStruct((m, V7X_LANES), F32)],
        compiler_params=pltpu.CompilerParams(
            dimension_semantics=("parallel",), vmem_limit_bytes=V7X_VMEM_LIMIT),
        name="xcast_if",
    )(x2, w_t.reshape(-1, k), bias)


def _xattn_body(q_ref, k_ref, v_ref, z_ref, o_ref):
    scale = X_HEAD_DIM ** -0.5
    for h in range(X_HEADS):
        sl = slice(h * X_HEAD_DIM, (h + 1) * X_HEAD_DIM)
        s = lax.dot_general(q_ref[:, sl], k_ref[:, sl], (((1,), (1,)), ((), ())),
                            preferred_element_type=F32) * scale
        e = jnp.exp(s - jnp.max(s, axis=-1, keepdims=True))
        p = e / jnp.sum(e, axis=-1, keepdims=True)
        a = jnp.dot(p.astype(BF16), v_ref[:, sl], preferred_element_type=F32)
        o_ref[:, sl] = (a * z_ref[:, sl].astype(F32)).astype(o_ref.dtype)


def _xattn(xq, sxz, kv, *, batch, seq, mem_len, bs):
    nb = seq // bs
    return pl.pallas_call(
        _xattn_body,
        grid=(batch, nb),
        in_specs=[pl.BlockSpec((bs, X_W), lambda b, i: (b * nb + i, 0)),
                  pl.BlockSpec((mem_len, X_W), lambda b, i: (b, 0)),
                  pl.BlockSpec((mem_len, X_W), lambda b, i: (b, 1)),
                  pl.BlockSpec((bs, X_W), lambda b, i: (b * nb + i, 0))],
        out_specs=pl.BlockSpec((bs, X_W), lambda b, i: (b * nb + i, 0)),
        out_shape=jax.ShapeDtypeStruct((batch * seq, X_W), BF16),
        compiler_params=pltpu.CompilerParams(
            dimension_semantics=("parallel", "arbitrary"), vmem_limit_bytes=V7X_VMEM_LIMIT),
        name="xattn",
    )(xq, kv, kv, sxz)


def _mlstm_body(q_ref, k_ref, v_ref, g_ref, gate_ref, nw_ref, o_ref, c_ref, n_ref, m_ref):
    @pl.when(pl.program_id(2) == 0)
    def _():
        c_ref[...] = jnp.zeros_like(c_ref)
        n_ref[...] = jnp.zeros_like(n_ref)
        m_ref[...] = jnp.zeros_like(m_ref)

    for hg in range(MLSTM_HEAD_GROUP):
        qk_cols = slice(hg * M_QK_DIM, (hg + 1) * M_QK_DIM)
        v_cols = slice(hg * M_V_DIM, (hg + 1) * M_V_DIM)
        _mlstm_head(pl.program_id(1) * MLSTM_HEAD_GROUP + hg,
                    q_ref.at[:, qk_cols], k_ref.at[:, qk_cols], v_ref.at[:, v_cols], g_ref,
                    gate_ref.at[:, v_cols], nw_ref.at[:, v_cols], o_ref.at[:, v_cols],
                    c_ref.at[hg], n_ref.at[hg], m_ref.at[hg])


def _mlstm_head(h, q_ref, k_ref, v_ref, g_ref, gate_ref, nw_ref, o_ref, c_ref, n_ref, m_ref):
    L = MLSTM_CHUNK
    scale = M_QK_DIM ** -0.5
    q = q_ref[...]
    k = k_ref[...]
    v = v_ref[...]
    ig = g_ref[pl.ds(h, 1), :]
    lf = _log_sigmoid(g_ref[pl.ds(M_HEADS + h, 1), :])

    t_idx = lax.broadcasted_iota(jnp.int32, (L, L), 0)
    j_idx = lax.broadcasted_iota(jnp.int32, (L, L), 1)
    causal = j_idx <= t_idx
    eye = j_idx == t_idx
    b_col = jnp.sum(jnp.where(causal, lf, 0.0), axis=1, keepdims=True)
    b_row = jnp.sum(jnp.where(eye, b_col, 0.0), axis=0, keepdims=True)
    i_col = jnp.sum(jnp.where(eye, ig, 0.0), axis=1, keepdims=True)
    g = jnp.sum(lf, axis=1, keepdims=True)
    m_prev = m_ref[:, 0:1]

    log_d = jnp.where(causal, b_col - b_row + ig, -jnp.inf)
    m_inter = b_col + m_prev
    m_t = jnp.maximum(jnp.max(log_d, axis=1, keepdims=True), m_inter)
    s = lax.dot_general(q, k, (((1,), (1,)), ((), ())), preferred_element_type=F32)
    s = s * scale * jnp.exp(log_d - m_t)
    inter = jnp.exp(m_inter - m_t) * scale
    c_st = c_ref[...]
    num = (jnp.dot(s.astype(BF16), v, preferred_element_type=F32)
           + inter * jnp.dot(q, c_st.astype(BF16), preferred_element_type=F32))
    qn = jnp.sum(q.astype(F32) * n_ref[...], axis=1, keepdims=True)
    den = jnp.sum(s, axis=1, keepdims=True) + inter * qn
    hh = num / jnp.maximum(jnp.abs(den), jnp.exp(-m_t))

    w_col = g - b_col + i_col
    m_new = jnp.maximum(g + m_prev, jnp.max(w_col, axis=0, keepdims=True))
    decay = jnp.exp(g + m_prev - m_new)
    wk = jnp.exp(w_col - m_new) * k.astype(F32)
    c_ref[...] = decay * c_st + lax.dot_general(
        wk.astype(BF16), v, (((0,), (0,)), ((), ())), preferred_element_type=F32)
    n_ref[...] = decay * n_ref[...] + jnp.sum(wk, axis=0, keepdims=True)
    m_ref[...] = jnp.broadcast_to(m_new, m_ref.shape)

    mu = jnp.mean(hh, axis=1, keepdims=True)
    hc = hh - mu
    var = jnp.mean(hc * hc, axis=1, keepdims=True)
    y = hc * lax.rsqrt(var + LN_EPS) * nw_ref[...] * gate_ref[...].astype(F32)
    o_ref[...] = y.astype(o_ref.dtype)


def _mlstm(qkv, gates_t, gate_m, norm_w, *, batch, seq):
    L = MLSTM_CHUNK
    nc = seq // L
    G = MLSTM_HEAD_GROUP
    qk_w = G * M_QK_DIM
    v_w = G * M_V_DIM
    kq = M_QK // qk_w
    kv = (2 * M_QK) // v_w
    return pl.pallas_call(
        _mlstm_body,
        grid=(batch, M_HEADS // G, nc),
        in_specs=[pl.BlockSpec((L, qk_w), lambda b, h, c: (b * nc + c, h)),
                  pl.BlockSpec((L, qk_w), lambda b, h, c: (b * nc + c, kq + h)),
                  pl.BlockSpec((L, v_w), lambda b, h, c: (b * nc + c, kv + h)),
                  pl.BlockSpec((None, 2 * M_HEADS, L), lambda b, h, c: (b, 0, c)),
                  pl.BlockSpec((L, v_w), lambda b, h, c: (b * nc + c, h)),
                  pl.BlockSpec((1, v_w), lambda b, h, c: (0, h))],
        out_specs=pl.BlockSpec((L, v_w), lambda b, h, c: (b * nc + c, h)),
        out_shape=jax.ShapeDtypeStruct((batch * seq, M_V), BF16),
        scratch_shapes=[pltpu.VMEM((G, M_QK_DIM, M_V_DIM), F32),
                        pltpu.VMEM((G, 1, M_QK_DIM), F32),
                        pltpu.VMEM((G, 1, V7X_LANES), F32)],
        compiler_params=pltpu.CompilerParams(
            dimension_semantics=("parallel", "parallel", "arbitrary"),
            vmem_limit_bytes=V7X_VMEM_LIMIT),
        name="mlstm",
    )(qkv, qkv, qkv, gates_t, gate_m, norm_w)


def _conv_body(p_ref, gc_ref, w_ref, o_ref):
    p = p_ref[...].astype(F32)
    row = lax.broadcasted_iota(jnp.int32, p.shape, 0)
    acc = p * w_ref[CONV_K - 1:CONV_K, :]
    for d in range(1, CONV_K):
        shifted = jnp.where(row >= d, pltpu.roll(p, d, 0), 0.0)
        acc = acc + shifted * w_ref[CONV_K - 1 - d:CONV_K - d, :]
    o_ref[...] = (acc * gc_ref[...].astype(F32)).astype(o_ref.dtype)


def _conv(p, gc, conv_w, *, batch, seq, bn):
    return pl.pallas_call(
        _conv_body,
        grid=(batch, CONV_WIDTH // bn),
        in_specs=[pl.BlockSpec((seq, bn), lambda b, j: (b, j)),
                  pl.BlockSpec((seq, bn), lambda b, j: (b, j)),
                  pl.BlockSpec((CONV_K, bn), lambda b, j: (0, j))],
        out_specs=pl.BlockSpec((seq, bn), lambda b, j: (b, j)),
        out_shape=jax.ShapeDtypeStruct((batch * seq, CONV_WIDTH), BF16),
        compiler_params=pltpu.CompilerParams(
            dimension_semantics=("parallel", "arbitrary"), vmem_limit_bytes=V7X_VMEM_LIMIT),
        name="conv",
    )(p, gc, conv_w)


def _merge_body(ym_ref, yc_ref, yx_ref, wm_ref, wc_ref, wx_ref, gm_ref, gc_ref, gx_ref, o_ref):
    acc = gm_ref[...].astype(F32) * jnp.dot(ym_ref[...], wm_ref[...], preferred_element_type=F32)
    acc = acc + gc_ref[...].astype(F32) * jnp.dot(yc_ref[...], wc_ref[...], preferred_element_type=F32)
    acc = acc + gx_ref[...].astype(F32) * jnp.dot(yx_ref[...], wx_ref[...], preferred_element_type=F32)
    o_ref[...] = acc.astype(o_ref.dtype)


def _merge(y_m, y_c, y_x, wm, wc, wx, gates, *, bm, bn):
    m = y_m.shape[0]
    nb = D_MODEL // bn
    return pl.pallas_call(
        _merge_body,
        grid=(m // bm, nb),
        in_specs=[pl.BlockSpec((bm, M_V), lambda i, j: (i, 0)),
                  pl.BlockSpec((bm, CONV_WIDTH), lambda i, j: (i, 0)),
                  pl.BlockSpec((bm, X_W), lambda i, j: (i, 0)),
                  pl.BlockSpec((M_V, bn), lambda i, j: (0, j)),
                  pl.BlockSpec((CONV_WIDTH, bn), lambda i, j: (0, j)),
                  pl.BlockSpec((X_W, bn), lambda i, j: (0, j)),
                  pl.BlockSpec((bm, bn), lambda i, j: (i, j)),
                  pl.BlockSpec((bm, bn), lambda i, j: (i, nb + j)),
                  pl.BlockSpec((bm, bn), lambda i, j: (i, 2 * nb + j))],
        out_specs=pl.BlockSpec((bm, bn), lambda i, j: (i, j)),
        out_shape=jax.ShapeDtypeStruct((m, D_MODEL), BF16),
        compiler_params=pltpu.CompilerParams(
            dimension_semantics=("parallel", "arbitrary"), vmem_limit_bytes=V7X_VMEM_LIMIT),
        name="merge",
    )(y_m, y_c, y_x, wm, wc, wx, gates, gates, gates)


def _out_body(mg_ref, w_ref, x_ref, lw_ref, lb_ref, o_ref, *, bn, alpha):
    j = pl.program_id(1)
    z = alpha * x_ref[...] + jnp.dot(mg_ref[...], w_ref[...], preferred_element_type=F32)
    o_ref[:, pl.ds(pl.multiple_of(j * bn, bn), bn)] = z

    @pl.when(j == pl.num_programs(1) - 1)
    def _():
        for r in range(0, o_ref.shape[0], LN_ROWS):
            rows = slice(r, r + LN_ROWS)
            zz = o_ref[rows, :]
            mu = jnp.mean(zz, axis=1, keepdims=True)
            zc = zz - mu
            var = jnp.mean(zc * zc, axis=1, keepdims=True)
            o_ref[rows, :] = zc * lax.rsqrt(var + LN_EPS) * lw_ref[...] + lb_ref[...]


def _out(merged, w_out, x2, ln_w, ln_b, *, bm, bn, alpha):
    m = merged.shape[0]
    return pl.pallas_call(
        functools.partial(_out_body, bn=bn, alpha=alpha),
        grid=(m // bm, D_MODEL // bn),
        in_specs=[pl.BlockSpec((bm, D_MODEL), lambda i, j: (i, 0)),
                  pl.BlockSpec((D_MODEL, bn), lambda i, j: (0, j)),
                  pl.BlockSpec((bm, bn), lambda i, j: (i, j)),
                  pl.BlockSpec((1, D_MODEL), lambda i, j: (0, 0)),
                  pl.BlockSpec((1, D_MODEL), lambda i, j: (0, 0))],
        out_specs=pl.BlockSpec((bm, D_MODEL), lambda i, j: (i, 0)),
        out_shape=jax.ShapeDtypeStruct((m, D_MODEL), F32),
        compiler_params=pltpu.CompilerParams(
            dimension_semantics=("parallel", "arbitrary"), vmem_limit_bytes=V7X_VMEM_LIMIT),
        name="outproj_ln",
    )(merged, w_out, x2, ln_w, ln_b)


def _layer(x, mem, w_in_t, b_in, layer, conv_w, mh_norm_w, w_mem_kv, w_proj_m, w_proj_c, w_proj_x, w_out,
           ln_w, ln_b, *, alpha):
    batch, seq, d = x.shape
    mem_len = mem.shape[1]
    tokens = batch * seq
    x2 = x.reshape(tokens, d)

    def bseg(off, width):
        return b_in[off:off + width].reshape(1, width)

    xb, if_pre = _xcast_if(x2, w_in_t, bseg(OFF_MI, V7X_LANES), layer, bm=512)

    def proj(offs, width, epilogue, out_dtypes, bn, name, bm=1024, **kw):
        return _proj(xb, w_in_t, [bseg(o, width) for o in offs], layer, offs, width, epilogue, out_dtypes,
                     transposed=True, bm=bm, bn=bn, name=name, **kw)

    ident = lambda u: (u,)
    qkv_w = 2 * M_QK + M_V
    qkv = _proj_ring(xb, w_in_t.reshape(-1, d), bseg(OFF_Q, qkv_w), layer * D_IN + OFF_Q, qkv_w, BF16,
                     bm=1024, bn=512, name="proj_qkv")
    (gate_m,) = proj([OFF_MO, OFF_MZ], M_V, lambda mo, mz: (_sigmoid(mo.astype(BF16)) * _silu(mz.astype(BF16)),), [BF16], 256,
                     "proj_mgate")
    (p_conv,) = proj([OFF_CC, OFF_CX], CONV_WIDTH, lambda cc, cx: (cc * cx,), [BF16], 256, "proj_conv_p")
    (g_conv,) = proj([OFF_CB, OFF_CZ], CONV_WIDTH, lambda cb, cz: (cb.astype(BF16) * _silu(cz.astype(BF16)),), [BF16], 256,
                     "proj_conv_g")
    xq, sxz = proj([OFF_XQ, OFF_XZ], X_W, lambda q, z: (q, _silu(z.astype(BF16))), [BF16, BF16], 256, "proj_xattn")
    gates, wm, wc, wx, wo = proj([OFF_G], N_BRANCH * D_MODEL, lambda u: (_sigmoid(u.astype(BF16)),), [BF16], 512,
                                 "proj_gates", casts=(w_proj_m, w_proj_c, w_proj_x, w_out))

    mem2 = mem.reshape(batch * mem_len, d)
    (kv,) = _proj(mem2, w_mem_kv, None, layer, [0], 2 * X_W, ident, [BF16], transposed=False,
                  bm=batch * mem_len, bn=512, name="proj_memkv", single_buffer_x=True)
    y_x = _xattn(xq, sxz, kv, batch=batch, seq=seq, mem_len=mem_len, bs=1024)

    n_if = 2 * M_HEADS
    gates_t = if_pre[:, :n_if].reshape(batch, seq, n_if).transpose(0, 2, 1)
    y_m = _mlstm(qkv, gates_t, gate_m, mh_norm_w.reshape(1, M_V), batch=batch, seq=seq)

    y_c = _conv(p_conv, g_conv, conv_w, batch=batch, seq=seq, bn=512)

    merged = _merge(y_m, y_c, y_x, wm, wc, wx, gates, bm=1024, bn=256)
    out = _out(merged, wo, x2, ln_w.reshape(1, d), ln_b.reshape(1, d), bm=512, bn=1024, alpha=alpha)
    return out.reshape(batch, seq, d)


def kernel(x, mem, w_in, b_in, conv_w, mh_norm_w, w_mem_kv, w_proj_m, w_proj_c, w_proj_x, w_out, ln_w, ln_b):
    depth = w_in.shape[0]
    alpha = (2 * depth) ** 0.25
    w_in_t = jnp.swapaxes(w_in, 1, 2)
    for l in range(depth):
        x = _layer(x, mem, w_in_t, b_in[l], l, conv_w[l], mh_norm_w[l], w_mem_kv, w_proj_m,
                   w_proj_c, w_proj_x, w_out, ln_w[l], ln_b[l], alpha=alpha)
    return x
```

```python
import functools

import jax
import jax.numpy as jnp
from jax import lax
from jax.experimental import pallas as pl
from jax.experimental.pallas import tpu as pltpu

F32 = jnp.float32
BF16 = jnp.bfloat16

D_MODEL = 4096
M_HEADS = 8
M_QK_DIM = 256
M_V_DIM = 512
M_QK = M_HEADS * M_QK_DIM
M_V = M_HEADS * M_V_DIM
CONV_WIDTH = 2048
CONV_K = 3
X_HEADS = 4
X_HEAD_DIM = 512
X_W = X_HEADS * X_HEAD_DIM
N_BRANCH = 3
LN_EPS = 1e-5

OFF_Q = 0
OFF_K = OFF_Q + M_QK
OFF_V = OFF_K + M_QK
OFF_MO = OFF_V + M_V
OFF_MZ = OFF_MO + M_V
OFF_MI = OFF_MZ + M_V
OFF_MF = OFF_MI + M_HEADS
OFF_CB = OFF_MF + M_HEADS
OFF_CC = OFF_CB + CONV_WIDTH
OFF_CX = OFF_CC + CONV_WIDTH
OFF_CZ = OFF_CX + CONV_WIDTH
OFF_XQ = OFF_CZ + CONV_WIDTH
OFF_XZ = OFF_XQ + X_W
OFF_G = OFF_XZ + X_W
D_IN = OFF_G + N_BRANCH * D_MODEL

V7X_LANES = 128
V7X_VMEM_LIMIT = 56 * 1024 * 1024
MLSTM_CHUNK = 256
LN_ROWS = 128
CAST_ROWS = 64
MLSTM_HEAD_GROUP = 8


def _sigmoid(x):
    return 0.5 * jnp.tanh(0.5 * x) + 0.5


def _silu(x):
    return x * _sigmoid(x)


def _log_sigmoid(x):
    return jnp.minimum(x, 0.0) - jnp.log1p(jnp.exp(-jnp.abs(x)))


def _proj_body(*refs, n_w, has_bias, transposed, epilogue, cast_runs, col_axis=1):
    n_cast = len(cast_runs)
    x_ref = refs[0]
    w_refs = refs[1:1 + n_w]
    n_b = n_w if has_bias else 0
    b_refs = refs[1 + n_w:1 + n_w + n_b]
    n_in = 1 + n_w + n_b
    cast_in = refs[n_in:n_in + n_cast]
    o_refs = refs[n_in + n_cast:len(refs) - n_cast]
    cast_out = refs[len(refs) - n_cast:]
    step = pl.program_id(0) * pl.num_programs(1) + pl.program_id(1)
    for src, dst, (first, count) in zip(cast_in, cast_out, cast_runs):
        @pl.when((step >= first) & (step < first + count))
        def _(src=src, dst=dst):
            dst[...] = src[...].astype(dst.dtype)

    x = x_ref[...].astype(BF16)
    contract = (((1,), (1,)), ((), ())) if transposed else (((1,), (0,)), ((), ()))
    us = []
    for idx, w in enumerate(w_refs):
        u = lax.dot_general(x, w[...].astype(BF16), contract, preferred_element_type=F32)
        if has_bias:
            col0 = pl.multiple_of(pl.program_id(col_axis) * u.shape[1], V7X_LANES)
            u = u + b_refs[idx][:, pl.ds(col0, u.shape[1])]
        us.append(u)
    for o, val in zip(o_refs, epilogue(*us)):
        o[...] = val.astype(o.dtype)


def _proj(x, w, biases, layer, col_offs, width, epilogue, out_dtypes, *, transposed, bm, bn, name,
          casts=(), single_buffer_x=False, weight_stationary=False):
    m, k = x.shape
    n_w = len(col_offs)
    n_j = width // bn
    x_mode = {"pipeline_mode": pl.Buffered(1)} if single_buffer_x else {}
    in_specs = [pl.BlockSpec((bm, k), lambda i, j: (i, 0), **x_mode)]
    if transposed:
        base = layer * w.shape[1]
        w = w.reshape(-1, k)
        in_specs += [pl.BlockSpec((pl.Element(bn), pl.Element(k)), lambda i, j, o=base + off: (pl.multiple_of(o + j * bn, 8), 0))
                     for off in col_offs]
    else:
        in_specs += [pl.BlockSpec((None, k, bn), lambda i, j, o=off // bn: (layer, 0, o + j))
                     for off in col_offs]
    operands = [x] + [w] * n_w
    if biases is not None:
        in_specs += [pl.BlockSpec((1, width), lambda i, j: (0, 0)) for _ in biases]
        operands += list(biases)
    out_specs = [pl.BlockSpec((bm, bn), lambda i, j: (i, j)) for _ in out_dtypes]
    out_shape = [jax.ShapeDtypeStruct((m, width), dt) for dt in out_dtypes]
    first_step = 0
    cast_runs = []
    for arr in casts:
        _, rows, cols = arr.shape
        n_slabs = rows // CAST_ROWS
        cast_runs.append((first_step, n_slabs))

        def slab(i, j, s=first_step, n=n_slabs):
            return jnp.clip(i * n_j + j - s, 0, n - 1)

        in_specs.append(pl.BlockSpec((None, CAST_ROWS, cols), lambda i, j, f=slab: (layer, f(i, j), 0)))
        out_specs.append(pl.BlockSpec((CAST_ROWS, cols), lambda i, j, f=slab: (f(i, j), 0)))
        out_shape.append(jax.ShapeDtypeStruct((rows, cols), BF16))
        operands.append(arr)
        first_step += n_slabs
    assert first_step <= (m // bm) * n_j, "not enough grid steps to walk the cast arrays"
    grid, col_axis = (m // bm, n_j), 1
    if weight_stationary:
        assert not casts

        def swapped(spec):
            return pl.BlockSpec(spec.block_shape, lambda j, i, f=spec.index_map: f(i, j))

        in_specs = [swapped(s) for s in in_specs]
        out_specs = [swapped(s) for s in out_specs]
        grid, col_axis = (n_j, m // bm), 0
    return pl.pallas_call(
        functools.partial(_proj_body, n_w=n_w, has_bias=biases is not None, transposed=transposed,
                          epilogue=epilogue, cast_runs=tuple(cast_runs), col_axis=col_axis),
        grid=grid, in_specs=in_specs, out_specs=out_specs, out_shape=out_shape,
        compiler_params=pltpu.CompilerParams(
            dimension_semantics=("parallel", "arbitrary"), vmem_limit_bytes=V7X_VMEM_LIMIT),
        name=name,
    )(*operands)


def _xcast_if_body(x_ref, w_ref, b_ref, xb_ref, if_ref):
    xb = x_ref[...].astype(BF16)
    xb_ref[...] = xb
    if_ref[...] = lax.dot_general(xb, w_ref[...].astype(BF16), (((1,), (1,)), ((), ())),
                                  preferred_element_type=F32) + b_ref[...]


def _xcast_if(x2, w_t, bias, layer, *, bm):
    m, k = x2.shape
    row0 = layer * w_t.shape[1] + OFF_MI
    return pl.pallas_call(
        _xcast_if_body,
        grid=(m // bm,),
        in_specs=[pl.BlockSpec((bm, k), lambda i: (i, 0)),
                  pl.BlockSpec((pl.Element(V7X_LANES), pl.Element(k)), lambda i: (row0, 0)),
                  pl.BlockSpec((1, V7X_LANES), lambda i: (0, 0))],
        out_specs=[pl.BlockSpec((bm, k), lambda i: (i, 0)),
                   pl.BlockSpec((bm, V7X_LANES), lambda i: (i, 0))],
        out_shape=[jax.ShapeDtypeStruct((m, k), BF16), jax.ShapeDtypeStruct((m, V7X_LANES), F32)],
        compiler_params=pltpu.CompilerParams(
            dimension_semantics=("parallel",), vmem_limit_bytes=V7X_VMEM_LIMIT),
        name="xcast_if",
    )(x2, w_t.reshape(-1, k), bias)


def _xattn_body(q_ref, k_ref, v_ref, z_ref, o_ref):
    scale = X_HEAD_DIM ** -0.5
    for h in range(X_HEADS):
        sl = slice(h * X_HEAD_DIM, (h + 1) * X_HEAD_DIM)
        s = lax.dot_general(q_ref[:, sl], k_ref[:, sl], (((1,), (1,)), ((), ())),
                            preferred_element_type=F32) * scale
        e = jnp.exp(s - jnp.max(s, axis=-1, keepdims=True))
        p = e / jnp.sum(e, axis=-1, keepdims=True)
        a = jnp.dot(p.astype(BF16), v_ref[:, sl], preferred_element_type=F32)
        o_ref[:, sl] = (a * z_ref[:, sl].astype(F32)).astype(o_ref.dtype)


def _xattn(xq, sxz, kv, *, batch, seq, mem_len, bs):
    nb = seq // bs
    return pl.pallas_call(
        _xattn_body,
        grid=(batch, nb),
        in_specs=[pl.BlockSpec((bs, X_W), lambda b, i: (b * nb + i, 0)),
                  pl.BlockSpec((mem_len, X_W), lambda b, i: (b, 0)),
                  pl.BlockSpec((mem_len, X_W), lambda b, i: (b, 1)),
                  pl.BlockSpec((bs, X_W), lambda b, i: (b * nb + i, 0))],
        out_specs=pl.BlockSpec((bs, X_W), lambda b, i: (b * nb + i, 0)),
        out_shape=jax.ShapeDtypeStruct((batch * seq, X_W), BF16),
        compiler_params=pltpu.CompilerParams(
            dimension_semantics=("parallel", "arbitrary"), vmem_limit_bytes=V7X_VMEM_LIMIT),
        name="xattn",
    )(xq, kv, kv, sxz)


def _mlstm_body(q_ref, k_ref, v_ref, g_ref, gate_ref, nw_ref, o_ref, c_ref, n_ref, m_ref):
    @pl.when(pl.program_id(2) == 0)
    def _():
        c_ref[...] = jnp.zeros_like(c_ref)
        n_ref[...] = jnp.zeros_like(n_ref)
        m_ref[...] = jnp.zeros_like(m_ref)

    for hg in range(MLSTM_HEAD_GROUP):
        qk_cols = slice(hg * M_QK_DIM, (hg + 1) * M_QK_DIM)
        v_cols = slice(hg * M_V_DIM, (hg + 1) * M_V_DIM)
        _mlstm_head(pl.program_id(1) * MLSTM_HEAD_GROUP + hg,
                    q_ref.at[:, qk_cols], k_ref.at[:, qk_cols], v_ref.at[:, v_cols], g_ref,
                    gate_ref.at[:, v_cols], nw_ref.at[:, v_cols], o_ref.at[:, v_cols],
                    c_ref.at[hg], n_ref.at[hg], m_ref.at[hg])


def _mlstm_head(h, q_ref, k_ref, v_ref, g_ref, gate_ref, nw_ref, o_ref, c_ref, n_ref, m_ref):
    L = MLSTM_CHUNK
    scale = M_QK_DIM ** -0.5
    q = q_ref[...]
    k = k_ref[...]
    v = v_ref[...]
    ig = g_ref[pl.ds(h, 1), :]
    lf = _log_sigmoid(g_ref[pl.ds(M_HEADS + h, 1), :])

    t_idx = lax.broadcasted_iota(jnp.int32, (L, L), 0)
    j_idx = lax.broadcasted_iota(jnp.int32, (L, L), 1)
    causal = j_idx <= t_idx
    eye = j_idx == t_idx
    b_col = jnp.sum(jnp.where(causal, lf, 0.0), axis=1, keepdims=True)
    b_row = jnp.sum(jnp.where(eye, b_col, 0.0), axis=0, keepdims=True)
    i_col = jnp.sum(jnp.where(eye, ig, 0.0), axis=1, keepdims=True)
    g = jnp.sum(lf, axis=1, keepdims=True)
    m_prev = m_ref[:, 0:1]

    log_d = jnp.where(causal, b_col - b_row + ig, -jnp.inf)
    m_inter = b_col + m_prev
    m_t = jnp.maximum(jnp.max(log_d, axis=1, keepdims=True), m_inter)
    s = lax.dot_general(q, k, (((1,), (1,)), ((), ())), preferred_element_type=F32)
    s = s * scale * jnp.exp(log_d - m_t)
    inter = jnp.exp(m_inter - m_t) * scale
    c_st = c_ref[...]
    num = (jnp.dot(s.astype(BF16), v, preferred_element_type=F32)
           + inter * jnp.dot(q, c_st.astype(BF16), preferred_element_type=F32))
    qn = jnp.sum(q.astype(F32) * n_ref[...], axis=1, keepdims=True)
    den = jnp.sum(s, axis=1, keepdims=True) + inter * qn
    hh = num / jnp.maximum(jnp.abs(den), jnp.exp(-m_t))

    w_col = g - b_col + i_col
    m_new = jnp.maximum(g + m_prev, jnp.max(w_col, axis=0, keepdims=True))
    decay = jnp.exp(g + m_prev - m_new)
    wk = jnp.exp(w_col - m_new) * k.astype(F32)
    c_ref[...] = decay * c_st + lax.dot_general(
        wk.astype(BF16), v, (((0,), (0,)), ((), ())), preferred_element_type=F32)
    n_ref[...] = decay * n_ref[...] + jnp.sum(wk, axis=0, keepdims=True)
    m_ref[...] = jnp.broadcast_to(m_new, m_ref.shape)

    mu = jnp.mean(hh, axis=1, keepdims=True)
    hc = hh - mu
    var = jnp.mean(hc * hc, axis=1, keepdims=True)
    y = hc * lax.rsqrt(var + LN_EPS) * nw_ref[...] * gate_ref[...].astype(F32)
    o_ref[...] = y.astype(o_ref.dtype)


def _mlstm(qkv, gates_t, gate_m, norm_w, *, batch, seq):
    L = MLSTM_CHUNK
    nc = seq // L
    G = MLSTM_HEAD_GROUP
    qk_w = G * M_QK_DIM
    v_w = G * M_V_DIM
    kq = M_QK // qk_w
    kv = (2 * M_QK) // v_w
    return pl.pallas_call(
        _mlstm_body,
        grid=(batch, M_HEADS // G, nc),
        in_specs=[pl.BlockSpec((L, qk_w), lambda b, h, c: (b * nc + c, h)),
                  pl.BlockSpec((L, qk_w), lambda b, h, c: (b * nc + c, kq + h)),
                  pl.BlockSpec((L, v_w), lambda b, h, c: (b * nc + c, kv + h)),
                  pl.BlockSpec((None, 2 * M_HEADS, L), lambda b, h, c: (b, 0, c)),
                  pl.BlockSpec((L, v_w), lambda b, h, c: (b * nc + c, h)),
                  pl.BlockSpec((1, v_w), lambda b, h, c: (0, h))],
        out_specs=pl.BlockSpec((L, v_w), lambda b, h, c: (b * nc + c, h)),
        out_shape=jax.ShapeDtypeStruct((batch * seq, M_V), BF16),
        scratch_shapes=[pltpu.VMEM((G, M_QK_DIM, M_V_DIM), F32),
                        pltpu.VMEM((G, 1, M_QK_DIM), F32),
                        pltpu.VMEM((G, 1, V7X_LANES), F32)],
        compiler_params=pltpu.CompilerParams(
            dimension_semantics=("parallel", "parallel", "arbitrary"),
            vmem_limit_bytes=V7X_VMEM_LIMIT),
        name="mlstm",
    )(qkv, qkv, qkv, gates_t, gate_m, norm_w)


def _conv_body(p_ref, gc_ref, w_ref, o_ref):
    p = p_ref[...].astype(F32)
    row = lax.broadcasted_iota(jnp.int32, p.shape, 0)
    acc = p * w_ref[CONV_K - 1:CONV_K, :]
    for d in range(1, CONV_K):
        shifted = jnp.where(row >= d, pltpu.roll(p, d, 0), 0.0)
        acc = acc + shifted * w_ref[CONV_K - 1 - d:CONV_K - d, :]
    o_ref[...] = (acc * gc_ref[...].astype(F32)).astype(o_ref.dtype)


def _conv(p, gc, conv_w, *, batch, seq, bn):
    return pl.pallas_call(
        _conv_body,
        grid=(batch, CONV_WIDTH // bn),
        in_specs=[pl.BlockSpec((seq, bn), lambda b, j: (b, j)),
                  pl.BlockSpec((seq, bn), lambda b, j: (b, j)),
                  pl.BlockSpec((CONV_K, bn), lambda b, j: (0, j))],
        out_specs=pl.BlockSpec((seq, bn), lambda b, j: (b, j)),
        out_shape=jax.ShapeDtypeStruct((batch * seq, CONV_WIDTH), BF16),
        compiler_params=pltpu.CompilerParams(
            dimension_semantics=("parallel", "arbitrary"), vmem_limit_bytes=V7X_VMEM_LIMIT),
        name="conv",
    )(p, gc, conv_w)


def _merge_body(ym_ref, yc_ref, yx_ref, wm_ref, wc_ref, wx_ref, gm_ref, gc_ref, gx_ref, o_ref):
    acc = gm_ref[...].astype(F32) * jnp.dot(ym_ref[...], wm_ref[...], preferred_element_type=F32)
    acc = acc + gc_ref[...].astype(F32) * jnp.dot(yc_ref[...], wc_ref[...], preferred_element_type=F32)
    acc = acc + gx_ref[...].astype(F32) * jnp.dot(yx_ref[...], wx_ref[...], preferred_element_type=F32)
    o_ref[...] = acc.astype(o_ref.dtype)


def _merge(y_m, y_c, y_x, wm, wc, wx, gates, *, bm, bn):
    m = y_m.shape[0]
    nb = D_MODEL // bn
    return pl.pallas_call(
        _merge_body,
        grid=(m // bm, nb),
        in_specs=[pl.BlockSpec((bm, M_V), lambda i, j: (i, 0)),
                  pl.BlockSpec((bm, CONV_WIDTH), lambda i, j: (i, 0)),
                  pl.BlockSpec((bm, X_W), lambda i, j: (i, 0)),
                  pl.BlockSpec((M_V, bn), lambda i, j: (0, j)),
                  pl.BlockSpec((CONV_WIDTH, bn), lambda i, j: (0, j)),
                  pl.BlockSpec((X_W, bn), lambda i, j: (0, j)),
                  pl.BlockSpec((bm, bn), lambda i, j: (i, j)),
                  pl.BlockSpec((bm, bn), lambda i, j: (i, nb + j)),
                  pl.BlockSpec((bm, bn), lambda i, j: (i, 2 * nb + j))],
        out_specs=pl.BlockSpec((bm, bn), lambda i, j: (i, j)),
        out_shape=jax.ShapeDtypeStruct((m, D_MODEL), BF16),
        compiler_params=pltpu.CompilerParams(
            dimension_semantics=("parallel", "arbitrary"), vmem_limit_bytes=V7X_VMEM_LIMIT),
        name="merge",
    )(y_m, y_c, y_x, wm, wc, wx, gates, gates, gates)


def _out_body(mg_ref, w_ref, x_ref, lw_ref, lb_ref, o_ref, *, bn, alpha):
    j = pl.program_id(1)
    z = alpha * x_ref[...] + jnp.dot(mg_ref[...], w_ref[...], preferred_element_type=F32)
    o_ref[:, pl.ds(pl.multiple_of(j * bn, bn), bn)] = z

    @pl.when(j == pl.num_programs(1) - 1)
    def _():
        for r in range(0, o_ref.shape[0], LN_ROWS):
            rows = slice(r, r + LN_ROWS)
            zz = o_ref[rows, :]
            mu = jnp.mean(zz, axis=1, keepdims=True)
            zc = zz - mu
            var = jnp.mean(zc * zc, axis=1, keepdims=True)
            o_ref[rows, :] = zc * lax.rsqrt(var + LN_EPS) * lw_ref[...] + lb_ref[...]


def _out(merged, w_out, x2, ln_w, ln_b, *, bm, bn, alpha):
    m = merged.shape[0]
    return pl.pallas_call(
        functools.partial(_out_body, bn=bn, alpha=alpha),
        grid=(m // bm, D_MODEL // bn),
        in_specs=[pl.BlockSpec((bm, D_MODEL), lambda i, j: (i, 0)),
                  pl.BlockSpec((D_MODEL, bn), lambda i, j: (0, j)),
                  pl.BlockSpec((bm, bn), lambda i, j: (i, j)),
                  pl.BlockSpec((1, D_MODEL), lambda i, j: (0, 0)),
                  pl.BlockSpec((1, D_MODEL), lambda i, j: (0, 0))],
        out_specs=pl.BlockSpec((bm, D_MODEL), lambda i, j: (i, 0)),
        out_shape=jax.ShapeDtypeStruct((m, D_MODEL), F32),
        compiler_params=pltpu.CompilerParams(
            dimension_semantics=("parallel", "arbitrary"), vmem_limit_bytes=V7X_VMEM_LIMIT),
        name="outproj_ln",
    )(merged, w_out, x2, ln_w, ln_b)


def _layer(x, mem, w_in_t, b_in, layer, conv_w, mh_norm_w, w_mem_kv, w_proj_m, w_proj_c, w_proj_x, w_out,
           ln_w, ln_b, *, alpha):
    batch, seq, d = x.shape
    mem_len = mem.shape[1]
    tokens = batch * seq
    x2 = x.reshape(tokens, d)

    def bseg(off, width):
        return b_in[off:off + width].reshape(1, width)

    xb, if_pre = _xcast_if(x2, w_in_t, bseg(OFF_MI, V7X_LANES), layer, bm=512)

    def proj(offs, width, epilogue, out_dtypes, bn, name, bm=1024, **kw):
        return _proj(xb, w_in_t, [bseg(o, width) for o in offs], layer, offs, width, epilogue, out_dtypes,
                     transposed=True, bm=bm, bn=bn, name=name, **kw)

    ident = lambda u: (u,)
    (qkv,) = proj([OFF_Q], 2 * M_QK + M_V, ident, [BF16], 512, "proj_qkv", bm=2048, single_buffer_x=True)
    (gate_m,) = proj([OFF_MO, OFF_MZ], M_V, lambda mo, mz: (_sigmoid(mo.astype(BF16)) * _silu(mz.astype(BF16)),), [BF16], 256,
                     "proj_mgate")
    (p_conv,) = proj([OFF_CC, OFF_CX], CONV_WIDTH, lambda cc, cx: (cc * cx,), [BF16], 256, "proj_conv_p",
                      weight_stationary=True)
    (g_conv,) = proj([OFF_CB, OFF_CZ], CONV_WIDTH, lambda cb, cz: (cb.astype(BF16) * _silu(cz.astype(BF16)),), [BF16], 256,
                     "proj_conv_g")
    xq, sxz = proj([OFF_XQ, OFF_XZ], X_W, lambda q, z: (q, _silu(z.astype(BF16))), [BF16, BF16], 256, "proj_xattn")
    gates, wm, wc, wx, wo = proj([OFF_G], N_BRANCH * D_MODEL, lambda u: (_sigmoid(u.astype(BF16)),), [BF16], 512,
                                 "proj_gates", casts=(w_proj_m, w_proj_c, w_proj_x, w_out))

    mem2 = mem.reshape(batch * mem_len, d)
    (kv,) = _proj(mem2, w_mem_kv, None, layer, [0], 2 * X_W, ident, [BF16], transposed=False,
                  bm=batch * mem_len, bn=512, name="proj_memkv", single_buffer_x=True)
    y_x = _xattn(xq, sxz, kv, batch=batch, seq=seq, mem_len=mem_len, bs=1024)

    n_if = 2 * M_HEADS
    gates_t = if_pre[:, :n_if].reshape(batch, seq, n_if).transpose(0, 2, 1)
    y_m = _mlstm(qkv, gates_t, gate_m, mh_norm_w.reshape(1, M_V), batch=batch, seq=seq)

    y_c = _conv(p_conv, g_conv, conv_w, batch=batch, seq=seq, bn=512)

    merged = _merge(y_m, y_c, y_x, wm, wc, wx, gates, bm=1024, bn=256)
    out = _out(merged, wo, x2, ln_w.reshape(1, d), ln_b.reshape(1, d), bm=512, bn=1024, alpha=alpha)
    return out.reshape(batch, seq, d)


def kernel(x, mem, w_in, b_in, conv_w, mh_norm_w, w_mem_kv, w_proj_m, w_proj_c, w_proj_x, w_out, ln_w, ln_b):
    depth = w_in.shape[0]
    alpha = (2 * depth) ** 0.25
    w_in_t = jnp.swapaxes(w_in, 1, 2)
    for l in range(depth):
        x = _layer(x, mem, w_in_t, b_in[l], l, conv_w[l], mh_norm_w[l], w_mem_kv, w_proj_m,
                   w_proj_c, w_proj_x, w_out, ln_w[l], ln_b[l], alpha=alpha)
    return x
```
